```python
import jax, jax.numpy as jnp
from jax import lax
import numpy as np

D_MODEL = 1024
BATCH = 32
SEQ = 2048
DEPTH = 1

CHUNK = 64
PLE_DIM = 256
D_MIX = D_MODEL
D_CONV = D_MIX // 2
D_LRU = D_MIX - D_CONV
CONV_HEADS = 8
LRU_HEADS = 8
LRU_HEAD_DIM = D_LRU // LRU_HEADS
CONV_WIDTH = 31
LRU_CONV_WIDTH = 4
LRU_C = 8.0
N_GROUPS = 4
EXPERTS_PER_GROUP = 8
N_EXPERTS = N_GROUPS * EXPERTS_PER_GROUP
TOP_K = 2
D_EXPERT = D_MODEL // 2
ROUTE_BLOCK = 128
EPS = 1e-6

kernel_name = 'hymba_conformer_rglru_hmoe_block'


def _rmsnorm(x, g):
    xf = x.astype(jnp.float32)
    y = xf * lax.rsqrt(jnp.mean(xf * xf, axis=-1, keepdims=True) + EPS)
    return (y * g.astype(jnp.float32)).astype(x.dtype)


def _layernorm(x, g, b):
    xf = x.astype(jnp.float32)
    mu = jnp.mean(xf, axis=-1, keepdims=True)
    var = jnp.mean(jnp.square(xf - mu), axis=-1, keepdims=True)
    y = (xf - mu) * lax.rsqrt(var + EPS)
    return (y * g.astype(jnp.float32) + b.astype(jnp.float32)).astype(x.dtype)


def _causal_depthwise_conv(x, w, b):
    width = w.shape[0]
    y = lax.conv_general_dilated(
        x, w[:, None, :], window_strides=(1,), padding=((width - 1, 0),),
        dimension_numbers=('NWC', 'WIO', 'NWC'), feature_group_count=x.shape[-1])
    return y + b


def _conformer_conv(v, g, w_dw, b_dw, ln_g, ln_b):
    u = v * jax.nn.sigmoid(g)
    u = _causal_depthwise_conv(u, w_dw, b_dw)
    return jax.nn.silu(_layernorm(u, ln_g, ln_b))


def _rglru(x, w_r, b_r, w_i, b_i, lam):
    B, S, C = x.shape
    xh = x.reshape(B, S, LRU_HEADS, LRU_HEAD_DIM)
    r = jax.nn.sigmoid(jnp.einsum('bshi,hij->bshj', xh, w_r).reshape(B, S, C) + b_r)
    i_g = jax.nn.sigmoid(jnp.einsum('bshi,hij->bshj', xh, w_i).reshape(B, S, C) + b_i)
    log_a = -LRU_C * r.astype(jnp.float32) * jax.nn.softplus(-lam.astype(jnp.float32))
    a = jnp.exp(log_a)
    u = jnp.sqrt(-jnp.expm1(2.0 * log_a)) * (i_g * x).astype(jnp.float32)
    nc = S // CHUNK
    a_c = a.reshape(B, nc, CHUNK, C)
    u_c = u.reshape(B, nc, CHUNK, C)

    def combine(left, right):
        a_l, h_l = left
        a_r, h_r = right
        return a_l * a_r, a_r * h_l + h_r

    a_cum, h_loc = lax.associative_scan(combine, (a_c, u_c), axis=2)

    def step(h_prev, xs):
        ac, hl = xs
        h = hl + ac * h_prev[:, None, :]
        return h[:, -1, :], h

    _, h = lax.scan(step, jnp.zeros((B, C), jnp.float32),
                    (jnp.moveaxis(a_cum, 1, 0), jnp.moveaxis(h_loc, 1, 0)))
    return jnp.moveaxis(h, 0, 1).reshape(B, S, C).astype(x.dtype)


def _hier_route(h, w_group, b_group, w_expert, b_expert):
    T = h.shape[0]
    g_logits = (h @ w_group).astype(jnp.float32) + b_group.astype(jnp.float32)
    g_prob, g_idx = lax.top_k(jax.nn.softmax(g_logits, axis=-1), 1)
    e_logits = ((h @ w_expert).astype(jnp.float32) + b_expert.astype(jnp.float32))
    e_logits = e_logits.reshape(T, N_GROUPS, EXPERTS_PER_GROUP)
    e_sel = jnp.take_along_axis(e_logits, g_idx[:, :, None], axis=1)[:, 0]
    e_prob, e_loc = lax.top_k(jax.nn.softmax(e_sel, axis=-1), TOP_K)
    e_prob = e_prob / jnp.sum(e_prob, axis=-1, keepdims=True)
    gate = g_prob * e_prob
    expert_id = g_idx * EXPERTS_PER_GROUP + e_loc
    return expert_id, gate


def _routed_experts(h, expert_id, gate, w1, w3, w2):
    T, D = h.shape
    A = T * TOP_K
    flat_e = expert_id.reshape(A)
    flat_tok = jnp.repeat(jnp.arange(T, dtype=jnp.int32), TOP_K, total_repeat_length=A)
    flat_gate = gate.reshape(A)
    order = jnp.argsort(flat_e)
    se, st, sg = flat_e[order], flat_tok[order], flat_gate[order]
    counts = jnp.zeros((N_EXPERTS,), jnp.int32).at[flat_e].add(1)
    padded = (counts + ROUTE_BLOCK - 1) // ROUTE_BLOCK * ROUTE_BLOCK
    start = jnp.cumsum(counts) - counts
    pad_end = jnp.cumsum(padded)
    pad_start = pad_end - padded
    dest = pad_start[se] + (jnp.arange(A, dtype=jnp.int32) - start[se])
    n_blocks = -(-A // ROUTE_BLOCK) + N_EXPERTS
    P = n_blocks * ROUTE_BLOCK
    buf_tok = jnp.zeros((P,), jnp.int32).at[dest].set(st)
    buf_gate = jnp.zeros((P,), jnp.float32).at[dest].set(sg)
    block_start = jnp.arange(n_blocks, dtype=jnp.int32) * ROUTE_BLOCK
    block_e = jnp.minimum(jnp.searchsorted(pad_end, block_start, side='right'), N_EXPERTS - 1)

    def expert_block(args):
        tok, e, g = args
        xb = h[tok]
        y = (jax.nn.silu(xb @ w1[e]) * (xb @ w3[e])) @ w2[e]
        return (y * g[:, None]).astype(h.dtype)

    y = lax.map(expert_block, (buf_tok.reshape(n_blocks, ROUTE_BLOCK), block_e,
                               buf_gate.reshape(n_blocks, ROUTE_BLOCK)))
    return jnp.zeros_like(h).at[buf_tok].add(y.reshape(P, D))


def setup_inputs(seed: int = 0) -> dict:
    key = jax.random.key(seed)
    ks = jax.random.split(key, 32)
    f32 = jnp.float32
    L = DEPTH

    def nrm(k, shape, scale):
        return jax.random.normal(k, shape, f32) * scale

    def gain(k, shape):
        return 1.0 + 0.05 * jax.random.normal(k, shape, f32)

    u = jax.random.uniform(ks[14], (L, D_LRU), f32, 0.9, 0.999)
    s = u ** (1.0 / LRU_C)
    lru_lambda = jnp.log(s) - jnp.log1p(-s)
    return {
        'x': nrm(ks[0], (BATCH, SEQ, D_MODEL), 1.0),
        'p': nrm(ks[1], (L, BATCH, SEQ, PLE_DIM), 1.0),
        'g_mix': gain(ks[2], (L, D_MODEL)),
        'w_in': nrm(ks[3], (L, D_MODEL, 2 * D_CONV + 2 * D_LRU), D_MODEL ** -0.5),
        'conv_dw_w': nrm(ks[4], (L, CONV_WIDTH, D_CONV), CONV_WIDTH ** -0.5),
        'conv_dw_b': nrm(ks[5], (L, D_CONV), 0.02),
        'conv_ln_g': gain(ks[6], (L, D_CONV)),
        'conv_ln_b': nrm(ks[7], (L, D_CONV), 0.02),
        'lru_conv_w': nrm(ks[8], (L, LRU_CONV_WIDTH, D_LRU), LRU_CONV_WIDTH ** -0.5),
        'lru_conv_b': nrm(ks[9], (L, D_LRU), 0.02),
        'lru_w_r': nrm(ks[10], (L, LRU_HEADS, LRU_HEAD_DIM, LRU_HEAD_DIM), LRU_HEAD_DIM ** -0.5),
        'lru_b_r': nrm(ks[11], (L, D_LRU), 0.02),
        'lru_w_i': nrm(ks[12], (L, LRU_HEADS, LRU_HEAD_DIM, LRU_HEAD_DIM), LRU_HEAD_DIM ** -0.5),
        'lru_b_i': nrm(ks[13], (L, D_LRU), 0.02),
        'lru_lambda': lru_lambda,
        'w_out': nrm(ks[15], (L, D_MIX, D_MODEL), D_MIX ** -0.5),
        'g_ffn': gain(ks[16], (L, D_MODEL)),
        'w_group': nrm(ks[17], (L, D_MODEL, N_GROUPS), D_MODEL ** -0.5),
        'b_group': nrm(ks[18], (L, N_GROUPS), 0.01),
        'w_expert': nrm(ks[19], (L, D_MODEL, N_EXPERTS), D_MODEL ** -0.5),
        'b_expert': nrm(ks[20], (L, N_EXPERTS), 0.01),
        'w1': nrm(ks[21], (L, N_EXPERTS, D_MODEL, D_EXPERT), D_MODEL ** -0.5),
        'w3': nrm(ks[22], (L, N_EXPERTS, D_MODEL, D_EXPERT), D_MODEL ** -0.5),
        'w2': nrm(ks[23], (L, N_EXPERTS, D_EXPERT, D_MODEL), D_EXPERT ** -0.5),
        'g_ple': gain(ks[24], (L, D_MODEL)),
        'w_ple': nrm(ks[25], (L, PLE_DIM, D_MODEL), PLE_DIM ** -0.5),
        'g_ple_proj': gain(ks[26], (L, D_MODEL)),
        'w_ple_gate': nrm(ks[27], (L, D_MODEL, D_MODEL), D_MODEL ** -0.5),
        'g_final': gain(ks[28], (D_MODEL,)),
    }


def reference(x, p, g_mix, w_in, conv_dw_w, conv_dw_b, conv_ln_g, conv_ln_b,
              lru_conv_w, lru_conv_b, lru_w_r, lru_b_r, lru_w_i, lru_b_i, lru_lambda,
              w_out, g_ffn, w_group, b_group, w_expert, b_expert, w1, w3, w2,
              g_ple, w_ple, g_ple_proj, w_ple_gate, g_final):
    B, S, D = x.shape
    for i in range(DEPTH):
        h = _rmsnorm(x, g_mix[i])
        z = h @ w_in[i]
        conv_v, conv_g, lru_x, lru_g = jnp.split(
            z, [D_CONV, 2 * D_CONV, 2 * D_CONV + D_LRU], axis=-1)
        y_conv = _conformer_conv(conv_v, conv_g, conv_dw_w[i], conv_dw_b[i],
                                 conv_ln_g[i], conv_ln_b[i])
        xr = _causal_depthwise_conv(lru_x, lru_conv_w[i], lru_conv_b[i])
        y_lru = _rglru(xr, lru_w_r[i], lru_b_r[i], lru_w_i[i], lru_b_i[i],
                       lru_lambda[i]) * jax.nn.gelu(lru_g)
        x = x + jnp.concatenate([y_conv, y_lru], axis=-1) @ w_out[i]
        hf = _rmsnorm(x, g_ffn[i]).reshape(B * S, D)
        eid, gate = _hier_route(hf, w_group[i], b_group[i], w_expert[i], b_expert[i])
        x = x + _routed_experts(hf, eid, gate, w1[i], w3[i], w2[i]).reshape(B, S, D)
        e = _rmsnorm(p[i] @ w_ple[i], g_ple_proj[i])
        pg = jax.nn.sigmoid(_rmsnorm(x, g_ple[i]) @ w_ple_gate[i])
        x = x + pg * e
    return _rmsnorm(x, g_final)
```

```python
import functools

import jax
import jax.numpy as jnp
from jax import lax
from jax.experimental import pallas as pl
from jax.experimental.pallas import tpu as pltpu

D_MODEL = 1024
D_CONV = 512
D_LRU = 512
LRU_HEADS = 8
LRU_HEAD_DIM = D_LRU // LRU_HEADS
CONV_WIDTH = 31
LRU_CONV_WIDTH = 4
LRU_C = 8.0
N_GROUPS = 4
EXPERTS_PER_GROUP = 8
N_EXPERTS = N_GROUPS * EXPERTS_PER_GROUP
TOP_K = 2
D_EXPERT = D_MODEL // 2
PLE_DIM = 256
EPS = 1e-6

SUBLANES = 8
LANES = 128

SEQ_TILE = 512
CONV_HIST = 32
LRU_HIST = SUBLANES
CONV_ROWS = 64
ROW_BLOCK = 512
DMA_TOKENS = 1024
TAIL_TILE = 512
ROUTER_COLS = LANES
EXPERT_COL0 = SUBLANES
VMEM_LIMIT = 56 * 1024 * 1024


def _sigmoid(v):
    return 1.0 / (1.0 + jnp.exp(-v))


def _rms(v, g):
    return v * lax.rsqrt(jnp.mean(v * v, axis=-1, keepdims=True) + EPS) * g


def _mixer_kernel(x_ref, gmix_ref, win_ref, cw_ref, cb_ref, lng_ref, lnb_ref,
                  lw_ref, lb_ref, wg_ref, bg_ref, lam_ref, wout_ref, gffn_ref,
                  wrh_ref, wrl_ref, br_ref, tri_ref,
                  x1_ref, hf_ref, eid_ref, gate_ref, rank_ref, cnt_ref,
                  uh_ref, lxh_ref, cv_ref, a_ref, h_ref, gl_ref, hcar_ref,
                  mix_ref, base_ref):
    ts = SEQ_TILE
    b = pl.program_id(0)
    s = pl.program_id(1)

    @pl.when(s == 0)
    def _():
        uh_ref[0:CONV_HIST, :] = jnp.zeros((CONV_HIST, D_CONV), jnp.float32)
        lxh_ref[0:LRU_HIST, :] = jnp.zeros((LRU_HIST, D_LRU), jnp.float32)
        hcar_ref[...] = jnp.zeros_like(hcar_ref)

    @pl.when((b == 0) & (s == 0))
    def _():
        base_ref[...] = jnp.zeros_like(base_ref)

    x = x_ref[...]
    hb = _rms(x, gmix_ref[...]).astype(jnp.bfloat16)

    def zcols(j):
        return jnp.dot(hb, win_ref[:, j * D_CONV:(j + 1) * D_CONV],
                       preferred_element_type=jnp.float32)

    uh_ref[CONV_HIST:CONV_HIST + ts, :] = zcols(0) * _sigmoid(zcols(1))
    lxh_ref[LRU_HIST:LRU_HIST + ts, :] = zcols(2)
    gl_ref[...] = jax.nn.gelu(zcols(3), approximate=True)

    for c in range(ts // CONV_ROWS):
        r0 = c * CONV_ROWS
        for lc in range(D_CONV // LANES):
            ls = slice(lc * LANES, (lc + 1) * LANES)
            acc = jnp.broadcast_to(cb_ref[:, ls], (CONV_ROWS, LANES))
            for j in range(CONV_WIDTH):
                k = CONV_WIDTH - 1 - j
                acc = acc + (uh_ref[r0 + CONV_HIST - j:r0 + CONV_HIST - j + CONV_ROWS, ls]
                             * cw_ref[k:k + 1, ls])
            cv_ref[r0:r0 + CONV_ROWS, ls] = acc

    cv = cv_ref[...]
    mu = jnp.mean(cv, axis=-1, keepdims=True)
    dv = cv - mu
    var = jnp.mean(dv * dv, axis=-1, keepdims=True)
    yc = dv * lax.rsqrt(var + EPS) * lng_ref[...] + lnb_ref[...]
    mix_ref[:, 0:D_CONV] = (yc * _sigmoid(yc)).astype(jnp.bfloat16)

    xr = jnp.broadcast_to(lb_ref[...], (ts, D_LRU))
    for j in range(LRU_CONV_WIDTH):
        k = LRU_CONV_WIDTH - 1 - j
        xr = xr + lxh_ref[LRU_HIST - j:LRU_HIST - j + ts, :] * lw_ref[k:k + 1, :]
    gates = jnp.dot(xr.astype(jnp.bfloat16), wg_ref[...],
                    preferred_element_type=jnp.float32) + bg_ref[...]
    r_g = _sigmoid(gates[:, 0:D_LRU])
    i_g = _sigmoid(gates[:, D_LRU:2 * D_LRU])
    nlam = -lam_ref[...]
    softplus = jnp.maximum(nlam, 0.0) + jnp.log1p(jnp.exp(-jnp.abs(nlam)))
    log_a = -LRU_C * r_g * softplus
    a_ref[...] = jnp.exp(log_a)
    th = jnp.tanh(log_a)
    h_ref[...] = jnp.sqrt(-2.0 * th / (1.0 - th)) * (i_g * xr)

    row = lax.broadcasted_iota(jnp.int32, (SUBLANES, D_LRU), 0)

    def scan_group(g, hprev):
        r0 = pl.multiple_of(g * SUBLANES, SUBLANES)
        a = a_ref[pl.ds(r0, SUBLANES), :]
        hh = h_ref[pl.ds(r0, SUBLANES), :]
        for sh in (1, 2, 4):
            keep = row >= sh
            a_sh = jnp.where(keep, pltpu.roll(a, sh, 0), 1.0)
            h_sh = jnp.where(keep, pltpu.roll(hh, sh, 0), 0.0)
            hh = a * h_sh + hh
            a = a * a_sh
        hh = hh + a * hprev
        h_ref[pl.ds(r0, SUBLANES), :] = hh
        return hh[SUBLANES - 1:SUBLANES, :]

    hcar_ref[...] = lax.fori_loop(0, ts // SUBLANES, scan_group, hcar_ref[...])
    mix_ref[:, D_CONV:D_CONV + D_LRU] = (h_ref[...] * gl_ref[...]).astype(jnp.bfloat16)

    uh_ref[0:CONV_HIST, :] = uh_ref[ts:ts + CONV_HIST, :]
    lxh_ref[0:LRU_HIST, :] = lxh_ref[ts:ts + LRU_HIST, :]

    x1 = x + jnp.dot(mix_ref[...], wout_ref[...], preferred_element_type=jnp.float32)
    x1_ref[...] = x1
    hf = _rms(x1, gffn_ref[...])
    hf_ref[...] = hf

    hf_hi = hf.astype(jnp.bfloat16)
    hf_lo = (hf - hf_hi.astype(jnp.float32)).astype(jnp.bfloat16)
    logits = (jnp.dot(hf_hi, wrh_ref[...], preferred_element_type=jnp.float32)
              + jnp.dot(hf_hi, wrl_ref[...], preferred_element_type=jnp.float32)
              + jnp.dot(hf_lo, wrh_ref[...], preferred_element_type=jnp.float32)
              + br_ref[...])
    lt = logits.T

    gl = [lt[g:g + 1, :] for g in range(N_GROUPS)]
    gmax = functools.reduce(jnp.maximum, gl)
    gidx = jnp.full((1, ts), N_GROUPS - 1, jnp.int32)
    for g in range(N_GROUPS - 2, -1, -1):
        gidx = jnp.where(gl[g] == gmax, g, gidx)
    gsum = functools.reduce(lambda p, q: p + q, [jnp.exp(v - gmax) for v in gl])
    gprob = 1.0 / gsum

    def eblock(g):
        lo = EXPERT_COL0 + g * EXPERTS_PER_GROUP
        return lt[lo:lo + EXPERTS_PER_GROUP, :]

    esel = eblock(N_GROUPS - 1)
    for g in range(N_GROUPS - 2, -1, -1):
        esel = jnp.where(gidx == g, eblock(g), esel)
    ridx = lax.broadcasted_iota(jnp.int32, (EXPERTS_PER_GROUP, ts), 0)
    m1 = jnp.max(esel, axis=0, keepdims=True)
    i1 = jnp.min(jnp.where(esel == m1, ridx, EXPERTS_PER_GROUP), axis=0, keepdims=True)
    rest = jnp.where(ridx == i1, -jnp.inf, esel)
    m2 = jnp.max(rest, axis=0, keepdims=True)
    i2 = jnp.min(jnp.where(rest == m2, ridx, EXPERTS_PER_GROUP), axis=0, keepdims=True)
    p2 = jnp.exp(m2 - m1)
    den = 1.0 + p2
    gate_ref[...] = jnp.concatenate(
        [gprob / den, gprob * p2 / den,
         jnp.zeros((SUBLANES - TOP_K, ts), jnp.float32)], axis=0)
    eid0 = gidx * EXPERTS_PER_GROUP + i1
    eid1 = gidx * EXPERTS_PER_GROUP + i2
    eid_ref[...] = jnp.concatenate([eid0, eid1], axis=0)

    eio = lax.broadcasted_iota(jnp.int32, (N_EXPERTS, ts), 0)
    oh0 = jnp.where(eio == eid0, 1.0, 0.0)
    oh1 = jnp.where(eio == eid1, 1.0, 0.0)
    tri = tri_ref[...]
    pre0 = jnp.dot(oh0.astype(jnp.bfloat16), tri, preferred_element_type=jnp.float32)
    pre1 = jnp.dot(oh1.astype(jnp.bfloat16), tri, preferred_element_type=jnp.float32)
    c0 = jnp.sum(oh0, axis=1, keepdims=True)
    c1 = jnp.sum(oh1, axis=1, keepdims=True)
    base = base_ref[:, 0:1]
    rank0 = jnp.sum(oh0 * (base + pre0), axis=0, keepdims=True)
    rank1 = jnp.sum(oh1 * (base + c0 + pre1), axis=0, keepdims=True)
    rank_ref[...] = jnp.concatenate([rank0, rank1], axis=0).astype(jnp.int32)
    total = jnp.broadcast_to(base + c0 + c1, (N_EXPERTS, LANES))
    base_ref[...] = total
    cnt_ref[...] = total


def _mixer(x2d, batch, seq, gmix, win, cw, cb, lng, lnb, lw, lb, wg, bg, lam, wout,
           gffn, wrh, wrl, br, tri):
    t = batch * seq
    ns = seq // SEQ_TILE
    tok = lambda b, s: (b * ns + s, 0)
    tokl = lambda b, s: (0, b * ns + s)
    const = lambda b, s: (0, 0)

    def full(a):
        return pl.BlockSpec(a.shape, const)

    weights = (gmix, win, cw, cb, lng, lnb, lw, lb, wg, bg, lam, wout, gffn, wrh, wrl, br, tri)
    return pl.pallas_call(
        _mixer_kernel,
        grid=(batch, ns),
        in_specs=[pl.BlockSpec((SEQ_TILE, D_MODEL), tok)] + [full(a) for a in weights],
        out_specs=[
            pl.BlockSpec((SEQ_TILE, D_MODEL), tok),
            pl.BlockSpec((SEQ_TILE, D_MODEL), tok),
            pl.BlockSpec((TOP_K, SEQ_TILE), tokl),
            pl.BlockSpec((SUBLANES, SEQ_TILE), tokl),
            pl.BlockSpec((TOP_K, SEQ_TILE), tokl),
            pl.BlockSpec((N_EXPERTS, LANES), const),
        ],
        out_shape=[
            jax.ShapeDtypeStruct((t, D_MODEL), jnp.float32),
            jax.ShapeDtypeStruct((t, D_MODEL), jnp.float32),
            jax.ShapeDtypeStruct((TOP_K, t), jnp.int32),
            jax.ShapeDtypeStruct((SUBLANES, t), jnp.float32),
            jax.ShapeDtypeStruct((TOP_K, t), jnp.int32),
            jax.ShapeDtypeStruct((N_EXPERTS, LANES), jnp.float32),
        ],
        scratch_shapes=[
            pltpu.VMEM((SEQ_TILE + CONV_HIST, D_CONV), jnp.float32),
            pltpu.VMEM((SEQ_TILE + LRU_HIST, D_LRU), jnp.float32),
            pltpu.VMEM((SEQ_TILE, D_CONV), jnp.float32),
            pltpu.VMEM((SEQ_TILE, D_LRU), jnp.float32),
            pltpu.VMEM((SEQ_TILE, D_LRU), jnp.float32),
            pltpu.VMEM((SEQ_TILE, D_LRU), jnp.float32),
            pltpu.VMEM((1, D_LRU), jnp.float32),
            pltpu.VMEM((SEQ_TILE, D_MODEL), jnp.bfloat16),
            pltpu.VMEM((N_EXPERTS, LANES), jnp.float32),
        ],
        compiler_params=pltpu.CompilerParams(
            dimension_semantics=("arbitrary", "arbitrary"),
            vmem_limit_bytes=VMEM_LIMIT),
        name="mixer_router",
    )(x2d, *weights)


def _rows_done(ref, rows, sem):
    return pltpu.make_async_copy(ref.at[pl.ds(0, rows)], ref.at[pl.ds(0, rows)], sem)


def _dispatch_kernel(ps_ref, eid_ref, rank_ref, hf_hbm, xs_init, xs_hbm, sem):
    del xs_init
    t0 = pl.program_id(0) * DMA_TOKENS

    def issue(t, carry):
        for k in range(TOP_K):
            dest = ps_ref[eid_ref[k, t]] + rank_ref[k, t]
            pltpu.make_async_copy(hf_hbm.at[pl.ds(t0 + t, 1)],
                                  xs_hbm.at[pl.ds(dest, 1)], sem).start()
        return carry

    lax.fori_loop(0, DMA_TOKENS, issue, 0)
    _rows_done(xs_hbm, TOP_K * DMA_TOKENS, sem).wait()


def _dispatch(pad_start, eid, rank, hf, n_rows):
    t = hf.shape[0]
    smem_tok = pl.BlockSpec((TOP_K, DMA_TOKENS), lambda i, ps: (0, i),
                            memory_space=pltpu.SMEM)
    return pl.pallas_call(
        _dispatch_kernel,
        grid_spec=pltpu.PrefetchScalarGridSpec(
            num_scalar_prefetch=1,
            grid=(t // DMA_TOKENS,),
            in_specs=[smem_tok, smem_tok,
                      pl.BlockSpec(memory_space=pl.ANY),
                      pl.BlockSpec(memory_space=pl.ANY)],
            out_specs=pl.BlockSpec(memory_space=pl.ANY),
            scratch_shapes=[pltpu.SemaphoreType.DMA(())]),
        out_shape=jax.ShapeDtypeStruct((n_rows, D_MODEL), jnp.float32),
        input_output_aliases={4: 0},
        compiler_params=pltpu.CompilerParams(dimension_semantics=("arbitrary",)),
        name="dispatch_rows",
    )(pad_start, eid, rank, hf, jnp.zeros((n_rows, D_MODEL), jnp.float32))


def _collect_kernel(ps_ref, eid_ref, rank_ref, y_hbm, ys_hbm, sem):
    t0 = pl.program_id(0) * DMA_TOKENS

    def issue(t, carry):
        for k in range(TOP_K):
            src = ps_ref[eid_ref[k, t]] + rank_ref[k, t]
            pltpu.make_async_copy(y_hbm.at[pl.ds(src, 1)],
                                  ys_hbm.at[k, pl.ds(t0 + t, 1)], sem).start()
        return carry

    lax.fori_loop(0, DMA_TOKENS, issue, 0)
    _rows_done(y_hbm, TOP_K * DMA_TOKENS, sem).wait()


def _collect(pad_start, eid, rank, y, t):
    smem_tok = pl.BlockSpec((TOP_K, DMA_TOKENS), lambda i, ps: (0, i),
                            memory_space=pltpu.SMEM)
    return pl.pallas_call(
        _collect_kernel,
        grid_spec=pltpu.PrefetchScalarGridSpec(
            num_scalar_prefetch=1,
            grid=(t // DMA_TOKENS,),
            in_specs=[smem_tok, smem_tok, pl.BlockSpec(memory_space=pl.ANY)],
            out_specs=pl.BlockSpec(memory_space=pl.ANY),
            scratch_shapes=[pltpu.SemaphoreType.DMA(())]),
        out_shape=jax.ShapeDtypeStruct((TOP_K, t, D_MODEL), jnp.float32),
        compiler_params=pltpu.CompilerParams(dimension_semantics=("arbitrary",)),
        name="collect_rows",
    )(pad_start, eid, rank, y)


def _expert_kernel(be_ref, nu_ref, xs_ref, w1_ref, w3_ref, w2_ref, y_ref):
    i = pl.program_id(0)

    @pl.when(i < nu_ref[0])
    def _():
        xb = xs_ref[...].astype(jnp.bfloat16)
        h1 = jnp.dot(xb, w1_ref[0], preferred_element_type=jnp.float32)
        h3 = jnp.dot(xb, w3_ref[0], preferred_element_type=jnp.float32)
        act = (h1 * _sigmoid(h1) * h3).astype(jnp.bfloat16)
        y_ref[...] = jnp.dot(act, w2_ref[0], preferred_element_type=jnp.float32)

    @pl.when(i >= nu_ref[0])
    def _():
        y_ref[...] = jnp.zeros_like(y_ref)


def _experts(block_e, n_used, xs, w1, w3, w2):
    n_rows = xs.shape[0]
    rows = lambda i, be, nu: (i, 0)
    wsel = lambda i, be, nu: (be[i], 0, 0)
    return pl.pallas_call(
        _expert_kernel,
        grid_spec=pltpu.PrefetchScalarGridSpec(
            num_scalar_prefetch=2,
            grid=(n_rows // ROW_BLOCK,),
            in_specs=[pl.BlockSpec((ROW_BLOCK, D_MODEL), rows),
                      pl.BlockSpec((1, D_MODEL, D_EXPERT), wsel),
                      pl.BlockSpec((1, D_MODEL, D_EXPERT), wsel),
                      pl.BlockSpec((1, D_EXPERT, D_MODEL), wsel)],
            out_specs=pl.BlockSpec((ROW_BLOCK, D_MODEL), rows)),
        out_shape=jax.ShapeDtypeStruct((n_rows, D_MODEL), jnp.float32),
        compiler_params=pltpu.CompilerParams(
            dimension_semantics=("arbitrary",), vmem_limit_bytes=VMEM_LIMIT),
        name="expert_mlp",
    )(block_e, n_used, xs, w1, w3, w2)


def _tail_kernel(x1_ref, ys_ref, gate_ref, p_ref, wple_ref, gproj_ref, gple_ref,
                 wpg_ref, gfin_ref, o_ref, *, final_norm):
    tt = TAIL_TILE
    gpad = jnp.concatenate(
        [gate_ref[...], jnp.zeros((LANES - SUBLANES, tt), jnp.float32)], axis=0)
    gt = gpad.T
    x2 = x1_ref[...] + gt[:, 0:1] * ys_ref[0] + gt[:, 1:2] * ys_ref[1]
    e = _rms(jnp.dot(p_ref[...].astype(jnp.bfloat16), wple_ref[...],
                     preferred_element_type=jnp.float32), gproj_ref[...])
    hg = _rms(x2, gple_ref[...]).astype(jnp.bfloat16)
    pg = _sigmoid(jnp.dot(hg, wpg_ref[...], preferred_element_type=jnp.float32))
    x3 = x2 + pg * e
    o_ref[...] = _rms(x3, gfin_ref[...]) if final_norm else x3


def _tail(x1, ys, gate, p2d, wple, gproj, gple, wpg, gfin, final_norm):
    t = x1.shape[0]
    tok = lambda i: (i, 0)
    const = lambda i: (0, 0)

    def full(a):
        return pl.BlockSpec(a.shape, const)

    return pl.pallas_call(
        functools.partial(_tail_kernel, final_norm=final_norm),
        grid=(t // TAIL_TILE,),
        in_specs=[pl.BlockSpec((TAIL_TILE, D_MODEL), tok),
                  pl.BlockSpec((TOP_K, TAIL_TILE, D_MODEL), lambda i: (0, i, 0)),
                  pl.BlockSpec((SUBLANES, TAIL_TILE), lambda i: (0, i)),
                  pl.BlockSpec((TAIL_TILE, PLE_DIM), tok),
                  full(wple), full(gproj), full(gple), full(wpg), full(gfin)],
        out_specs=pl.BlockSpec((TAIL_TILE, D_MODEL), tok),
        out_shape=jax.ShapeDtypeStruct((t, D_MODEL), jnp.float32),
        compiler_params=pltpu.CompilerParams(
            dimension_semantics=("arbitrary",), vmem_limit_bytes=VMEM_LIMIT),
        name="tail",
    )(x1, ys, gate, p2d, wple, gproj, gple, wpg, gfin)


def _block_diag(w):
    h, hd, _ = w.shape
    eye = jnp.eye(h, dtype=w.dtype)
    return jnp.einsum('hij,hg->higj', w, eye).reshape(h * hd, h * hd)


def _router_weights(w_group, b_group, w_expert, b_expert):
    w = jnp.zeros((D_MODEL, ROUTER_COLS), jnp.float32)
    w = w.at[:, 0:N_GROUPS].set(w_group)
    w = w.at[:, EXPERT_COL0:EXPERT_COL0 + N_EXPERTS].set(w_expert)
    bias = jnp.zeros((1, ROUTER_COLS), jnp.float32)
    bias = bias.at[0, 0:N_GROUPS].set(b_group)
    bias = bias.at[0, EXPERT_COL0:EXPERT_COL0 + N_EXPERTS].set(b_expert)
    w_hi = w.astype(jnp.bfloat16)
    w_lo = (w - w_hi.astype(jnp.float32)).astype(jnp.bfloat16)
    return w_hi, w_lo, bias


def _layer(x2d, p2d, batch, seq, final_norm, g_final, g_mix, w_in, conv_dw_w, conv_dw_b,
           conv_ln_g, conv_ln_b, lru_conv_w, lru_conv_b, lru_w_r, lru_b_r, lru_w_i, lru_b_i,
           lru_lambda, w_out, g_ffn, w_group, b_group, w_expert, b_expert, w1, w3, w2,
           g_ple, w_ple, g_ple_proj, w_ple_gate):
    bf16 = jnp.bfloat16
    t = batch * seq
    row = lambda v: v.reshape(1, -1)
    wg = jnp.concatenate([_block_diag(lru_w_r), _block_diag(lru_w_i)], axis=1).astype(bf16)
    bg = jnp.concatenate([lru_b_r, lru_b_i]).reshape(1, -1)
    wrh, wrl, br = _router_weights(w_group, b_group, w_expert, b_expert)
    ti = jnp.arange(SEQ_TILE)
    tri = (ti[:, None] < ti[None, :]).astype(bf16)

    x1, hf, eid, gate, rank, cnt = _mixer(
        x2d, batch, seq, row(g_mix), w_in.astype(bf16), conv_dw_w, row(conv_dw_b),
        row(conv_ln_g), row(conv_ln_b), lru_conv_w, row(lru_conv_b), wg, bg,
        row(lru_lambda), w_out.astype(bf16), row(g_ffn), wrh, wrl, br, tri)

    counts = cnt[:, 0].astype(jnp.int32)
    padded = (counts + ROW_BLOCK - 1) // ROW_BLOCK * ROW_BLOCK
    pad_end = jnp.cumsum(padded)
    pad_start = (pad_end - padded).astype(jnp.int32)
    n_blocks = (t * TOP_K) // ROW_BLOCK + N_EXPERTS
    n_rows = n_blocks * ROW_BLOCK
    block_start = jnp.arange(n_blocks, dtype=jnp.int32) * ROW_BLOCK
    block_e = jnp.sum((pad_end[None, :] <= block_start[:, None]).astype(jnp.int32), axis=1)
    block_e = jnp.minimum(block_e, N_EXPERTS - 1)
    n_used = (pad_end[-1:] // ROW_BLOCK).astype(jnp.int32)

    xs = _dispatch(pad_start, eid, rank, hf, n_rows)
    y = _experts(block_e, n_used, xs, w1.astype(bf16), w3.astype(bf16), w2.astype(bf16))
    ys = _collect(pad_start, eid, rank, y, t)
    return _tail(x1, ys, gate, p2d, w_ple.astype(bf16), row(g_ple_proj), row(g_ple),
                 w_ple_gate.astype(bf16), row(g_final), final_norm)


def kernel(x, p, g_mix, w_in, conv_dw_w, conv_dw_b, conv_ln_g, conv_ln_b, lru_conv_w,
           lru_conv_b, lru_w_r, lru_b_r, lru_w_i, lru_b_i, lru_lambda, w_out, g_ffn,
           w_group, b_group, w_expert, b_expert, w1, w3, w2, g_ple, w_ple, g_ple_proj,
           w_ple_gate, g_final):
    batch, seq, d = x.shape
    assert d == D_MODEL and seq % SEQ_TILE == 0
    assert (batch * seq) % max(DMA_TOKENS, TAIL_TILE, ROW_BLOCK) == 0
    depth = w_in.shape[0]
    layers = (g_mix, w_in, conv_dw_w, conv_dw_b, conv_ln_g, conv_ln_b, lru_conv_w,
              lru_conv_b, lru_w_r, lru_b_r, lru_w_i, lru_b_i, lru_lambda, w_out, g_ffn,
              w_group, b_group, w_expert, b_expert, w1, w3, w2, g_ple, w_ple, g_ple_proj,
              w_ple_gate)
    x2d = x.reshape(batch * seq, d)
    for i in range(depth):
        x2d = _layer(x2d, p[i].reshape(batch * seq, PLE_DIM), batch, seq, i == depth - 1,
                     g_final, *(w[i] for w in layers))
    return x2d.reshape(batch, seq, d)
```

```python
import functools

import jax
import jax.numpy as jnp
from jax import lax
from jax.experimental import pallas as pl
from jax.experimental.pallas import tpu as pltpu

D_MODEL = 1024
D_CONV = 512
D_LRU = 512
LRU_HEADS = 8
LRU_HEAD_DIM = D_LRU // LRU_HEADS
CONV_WIDTH = 31
LRU_CONV_WIDTH = 4
LRU_C = 8.0
N_GROUPS = 4
EXPERTS_PER_GROUP = 8
N_EXPERTS = N_GROUPS * EXPERTS_PER_GROUP
TOP_K = 2
D_EXPERT = D_MODEL // 2
PLE_DIM = 256
EPS = 1e-6

SUBLANES = 8
LANES = 128

SEQ_TILE = 512
CONV_HIST = 32
LRU_HIST = SUBLANES
CONV_ROWS = 64
ROW_BLOCK = 512
DMA_TOKENS = 1024
DMA_UNROLL = 8
DEST_TOKENS = 4096
TAIL_TILE = 512
ROUTER_COLS = LANES
EXPERT_COL0 = SUBLANES
VMEM_LIMIT = 56 * 1024 * 1024


def _sigmoid(v):
    return 1.0 / (1.0 + jnp.exp(-v))


def _rms(v, g):
    return v * lax.rsqrt(jnp.mean(v * v, axis=-1, keepdims=True) + EPS) * g


TOKEN_ROWS = D_MODEL // LANES


def _store_token_tiles(ref, v, row0=0):
    n = v.shape[0]
    for c in range(TOKEN_ROWS):
        ref[pl.ds(row0 + c, n, stride=TOKEN_ROWS), :] = v[:, c * LANES:(c + 1) * LANES]


def _load_token_tiles(ref, n, row0=0):
    return jnp.concatenate(
        [ref[pl.ds(row0 + c, n, stride=TOKEN_ROWS), :] for c in range(TOKEN_ROWS)], axis=1)


def _mixer_kernel(x_ref, gmix_ref, win_ref, cw_ref, cb_ref, lng_ref, lnb_ref,
                  lw_ref, lb_ref, wg_ref, bg_ref, lam_ref, wout_ref, gffn_ref,
                  wrh_ref, wrl_ref, br_ref, tri_ref,
                  x1_ref, hf_ref, eid_ref, gate_ref, rank_ref, cnt_ref,
                  uh_ref, ush_ref, lxh_ref, cv_ref, a_ref, h_ref, gl_ref, hcar_ref,
                  mix_ref, base_ref):
    ts = SEQ_TILE
    b = pl.program_id(0)
    s = pl.program_id(1)

    @pl.when(s == 0)
    def _():
        uh_ref[0:CONV_HIST, :] = jnp.zeros((CONV_HIST, D_CONV), jnp.float32)
        lxh_ref[0:LRU_HIST, :] = jnp.zeros((LRU_HIST, D_LRU), jnp.float32)
        hcar_ref[...] = jnp.zeros_like(hcar_ref)

    @pl.when((b == 0) & (s == 0))
    def _():
        base_ref[...] = jnp.zeros_like(base_ref)

    x = x_ref[...]
    hb = _rms(x, gmix_ref[...]).astype(jnp.bfloat16)

    def zcols(j):
        return jnp.dot(hb, win_ref[:, j * D_CONV:(j + 1) * D_CONV],
                       preferred_element_type=jnp.float32)

    uh_ref[CONV_HIST:CONV_HIST + ts, :] = zcols(0) * _sigmoid(zcols(1))
    lxh_ref[LRU_HIST:LRU_HIST + ts, :] = zcols(2)
    gl_ref[...] = jax.nn.gelu(zcols(3), approximate=True)

    span = ts + CONV_HIST - SUBLANES
    for r in range(SUBLANES):
        ush_ref[r] = uh_ref[SUBLANES - r:SUBLANES - r + span, :]
    for c in range(ts // CONV_ROWS):
        r0 = c * CONV_ROWS
        for lc in range(D_CONV // LANES):
            ls = slice(lc * LANES, (lc + 1) * LANES)
            acc = jnp.broadcast_to(cb_ref[:, ls], (CONV_ROWS, LANES))
            for j in range(CONV_WIDTH):
                k = CONV_WIDTH - 1 - j
                off = r0 + CONV_HIST - SUBLANES - SUBLANES * (j // SUBLANES)
                acc = acc + (ush_ref[j % SUBLANES, off:off + CONV_ROWS, ls]
                             * cw_ref[k:k + 1, ls])
            cv_ref[r0:r0 + CONV_ROWS, ls] = acc

    cv = cv_ref[...]
    mu = jnp.mean(cv, axis=-1, keepdims=True)
    dv = cv - mu
    var = jnp.mean(dv * dv, axis=-1, keepdims=True)
    yc = dv * lax.rsqrt(var + EPS) * lng_ref[...] + lnb_ref[...]
    mix_ref[:, 0:D_CONV] = (yc * _sigmoid(yc)).astype(jnp.bfloat16)

    xr = jnp.broadcast_to(lb_ref[...], (ts, D_LRU))
    for j in range(LRU_CONV_WIDTH):
        k = LRU_CONV_WIDTH - 1 - j
        xr = xr + lxh_ref[LRU_HIST - j:LRU_HIST - j + ts, :] * lw_ref[k:k + 1, :]
    gates = jnp.dot(xr.astype(jnp.bfloat16), wg_ref[...],
                    preferred_element_type=jnp.float32) + bg_ref[...]
    r_g = _sigmoid(gates[:, 0:D_LRU])
    i_g = _sigmoid(gates[:, D_LRU:2 * D_LRU])
    nlam = -lam_ref[...]
    softplus = jnp.maximum(nlam, 0.0) + jnp.log1p(jnp.exp(-jnp.abs(nlam)))
    log_a = -LRU_C * r_g * softplus
    a_ref[...] = jnp.exp(log_a)
    th = jnp.tanh(log_a)
    h_ref[...] = jnp.sqrt(-2.0 * th / (1.0 - th)) * (i_g * xr)

    row = lax.broadcasted_iota(jnp.int32, (SUBLANES, D_LRU), 0)

    def scan_group(g, hprev):
        r0 = pl.multiple_of(g * SUBLANES, SUBLANES)
        a = a_ref[pl.ds(r0, SUBLANES), :]
        hh = h_ref[pl.ds(r0, SUBLANES), :]
        for sh in (1, 2, 4):
            keep = row >= sh
            a_sh = jnp.where(keep, pltpu.roll(a, sh, 0), 1.0)
            h_sh = jnp.where(keep, pltpu.roll(hh, sh, 0), 0.0)
            hh = a * h_sh + hh
            a = a * a_sh
        hh = hh + a * hprev
        h_ref[pl.ds(r0, SUBLANES), :] = hh
        return hh[SUBLANES - 1:SUBLANES, :]

    hcar_ref[...] = lax.fori_loop(0, ts // SUBLANES, scan_group, hcar_ref[...])
    mix_ref[:, D_CONV:D_CONV + D_LRU] = (h_ref[...] * gl_ref[...]).astype(jnp.bfloat16)

    uh_ref[0:CONV_HIST, :] = uh_ref[ts:ts + CONV_HIST, :]
    lxh_ref[0:LRU_HIST, :] = lxh_ref[ts:ts + LRU_HIST, :]

    x1 = x + jnp.dot(mix_ref[...], wout_ref[...], preferred_element_type=jnp.float32)
    x1_ref[...] = x1
    hf = _rms(x1, gffn_ref[...])
    _store_token_tiles(hf_ref, hf)

    hf_hi = hf.astype(jnp.bfloat16)
    hf_lo = (hf - hf_hi.astype(jnp.float32)).astype(jnp.bfloat16)
    logits = (jnp.dot(hf_hi, wrh_ref[...], preferred_element_type=jnp.float32)
              + jnp.dot(hf_hi, wrl_ref[...], preferred_element_type=jnp.float32)
              + jnp.dot(hf_lo, wrh_ref[...], preferred_element_type=jnp.float32)
              + br_ref[...])
    lt = logits.T

    gl = [lt[g:g + 1, :] for g in range(N_GROUPS)]
    gmax = functools.reduce(jnp.maximum, gl)
    gidx = jnp.full((1, ts), N_GROUPS - 1, jnp.int32)
    for g in range(N_GROUPS - 2, -1, -1):
        gidx = jnp.where(gl[g] == gmax, g, gidx)
    gsum = functools.reduce(lambda p, q: p + q, [jnp.exp(v - gmax) for v in gl])
    gprob = 1.0 / gsum

    def eblock(g):
        lo = EXPERT_COL0 + g * EXPERTS_PER_GROUP
        return lt[lo:lo + EXPERTS_PER_GROUP, :]

    esel = eblock(N_GROUPS - 1)
    for g in range(N_GROUPS - 2, -1, -1):
        esel = jnp.where(gidx == g, eblock(g), esel)
    ridx = lax.broadcasted_iota(jnp.int32, (EXPERTS_PER_GROUP, ts), 0)
    m1 = jnp.max(esel, axis=0, keepdims=True)
    i1 = jnp.min(jnp.where(esel == m1, ridx, EXPERTS_PER_GROUP), axis=0, keepdims=True)
    rest = jnp.where(ridx == i1, -jnp.inf, esel)
    m2 = jnp.max(rest, axis=0, keepdims=True)
    i2 = jnp.min(jnp.where(rest == m2, ridx, EXPERTS_PER_GROUP), axis=0, keepdims=True)
    p2 = jnp.exp(m2 - m1)
    den = 1.0 + p2
    gate_ref[...] = jnp.concatenate(
        [gprob / den, gprob * p2 / den,
         jnp.zeros((SUBLANES - TOP_K, ts), jnp.float32)], axis=0)
    eid0 = gidx * EXPERTS_PER_GROUP + i1
    eid1 = gidx * EXPERTS_PER_GROUP + i2
    eid_ref[...] = jnp.concatenate([eid0, eid1], axis=0)

    eio = lax.broadcasted_iota(jnp.int32, (N_EXPERTS, ts), 0)
    oh0 = jnp.where(eio == eid0, 1.0, 0.0)
    oh1 = jnp.where(eio == eid1, 1.0, 0.0)
    tri = tri_ref[...]
    pre0 = jnp.dot(oh0.astype(jnp.bfloat16), tri, preferred_element_type=jnp.float32)
    pre1 = jnp.dot(oh1.astype(jnp.bfloat16), tri, preferred_element_type=jnp.float32)
    c0 = jnp.sum(oh0, axis=1, keepdims=True)
    c1 = jnp.sum(oh1, axis=1, keepdims=True)
    base = base_ref[:, 0:1]
    rank0 = jnp.sum(oh0 * (base + pre0), axis=0, keepdims=True)
    rank1 = jnp.sum(oh1 * (base + c0 + pre1), axis=0, keepdims=True)
    rank_ref[...] = jnp.concatenate([rank0, rank1], axis=0).astype(jnp.int32)
    total = jnp.broadcast_to(base + c0 + c1, (N_EXPERTS, LANES))
    base_ref[...] = total
    cnt_ref[...] = total


def _mixer(x2d, batch, seq, gmix, win, cw, cb, lng, lnb, lw, lb, wg, bg, lam, wout,
           gffn, wrh, wrl, br, tri):
    t = batch * seq
    ns = seq // SEQ_TILE
    tok = lambda b, s: (b * ns + s, 0)
    tokl = lambda b, s: (0, b * ns + s)
    const = lambda b, s: (0, 0)

    def full(a):
        return pl.BlockSpec(a.shape, const)

    weights = (gmix, win, cw, cb, lng, lnb, lw, lb, wg, bg, lam, wout, gffn, wrh, wrl, br, tri)
    return pl.pallas_call(
        _mixer_kernel,
        grid=(batch, ns),
        in_specs=[pl.BlockSpec((SEQ_TILE, D_MODEL), tok)] + [full(a) for a in weights],
        out_specs=[
            pl.BlockSpec((SEQ_TILE, D_MODEL), tok),
            pl.BlockSpec((SEQ_TILE * TOKEN_ROWS, LANES), tok),
            pl.BlockSpec((TOP_K, SEQ_TILE), tokl),
            pl.BlockSpec((SUBLANES, SEQ_TILE), tokl),
            pl.BlockSpec((TOP_K, SEQ_TILE), tokl),
            pl.BlockSpec((N_EXPERTS, LANES), const),
        ],
        out_shape=[
            jax.ShapeDtypeStruct((t, D_MODEL), jnp.float32),
            jax.ShapeDtypeStruct((t * TOKEN_ROWS, LANES), jnp.float32),
            jax.ShapeDtypeStruct((TOP_K, t), jnp.int32),
            jax.ShapeDtypeStruct((SUBLANES, t), jnp.float32),
            jax.ShapeDtypeStruct((TOP_K, t), jnp.int32),
            jax.ShapeDtypeStruct((N_EXPERTS, LANES), jnp.float32),
        ],
        scratch_shapes=[
            pltpu.VMEM((SEQ_TILE + CONV_HIST, D_CONV), jnp.float32),
            pltpu.VMEM((SUBLANES, SEQ_TILE + CONV_HIST - SUBLANES, D_CONV),
                       jnp.float32),
            pltpu.VMEM((SEQ_TILE + LRU_HIST, D_LRU), jnp.float32),
            pltpu.VMEM((SEQ_TILE, D_CONV), jnp.float32),
            pltpu.VMEM((SEQ_TILE, D_LRU), jnp.float32),
            pltpu.VMEM((SEQ_TILE, D_LRU), jnp.float32),
            pltpu.VMEM((SEQ_TILE, D_LRU), jnp.float32),
            pltpu.VMEM((1, D_LRU), jnp.float32),
            pltpu.VMEM((SEQ_TILE, D_MODEL), jnp.bfloat16),
            pltpu.VMEM((N_EXPERTS, LANES), jnp.float32),
        ],
        compiler_params=pltpu.CompilerParams(
            dimension_semantics=("arbitrary", "arbitrary"),
            vmem_limit_bytes=VMEM_LIMIT),
        name="mixer_router",
    )(x2d, *weights)


def _dest_kernel(ps_ref, eid_ref, rank_ref, dest_ref):
    eid = eid_ref[...]
    dest = rank_ref[...]
    for e in range(N_EXPERTS):
        dest = dest + jnp.where(eid == e, ps_ref[e], 0)
    dest_ref[...] = dest


def _dest_rows(pad_start, eid, rank):
    t = eid.shape[1]
    blk = pl.BlockSpec((TOP_K, DEST_TOKENS), lambda i, ps: (0, i))
    return pl.pallas_call(
        _dest_kernel,
        grid_spec=pltpu.PrefetchScalarGridSpec(
            num_scalar_prefetch=1, grid=(t // DEST_TOKENS,),
            in_specs=[blk, blk], out_specs=blk),
        out_shape=jax.ShapeDtypeStruct((TOP_K, t), jnp.int32),
        compiler_params=pltpu.CompilerParams(dimension_semantics=("arbitrary",)),
        name="dest_rows",
    )(pad_start, eid, rank)


def _token_tile(ref, tile):
    return ref.at[pl.ds(pl.multiple_of(tile * TOKEN_ROWS, TOKEN_ROWS), TOKEN_ROWS), :]


def _dispatch_kernel(dest_ref, hf_ref, xs_init, xs_hbm, sem):
    del xs_init

    def issue(t, carry):
        for k in range(TOP_K):
            pltpu.make_async_copy(_token_tile(hf_ref, t),
                                  _token_tile(xs_hbm, dest_ref[k, t]), sem).start()
        return carry

    lax.fori_loop(0, DMA_TOKENS, issue, 0, unroll=DMA_UNROLL)
    rows = TOP_K * DMA_TOKENS * TOKEN_ROWS
    pltpu.make_async_copy(xs_hbm.at[pl.ds(0, rows)], xs_hbm.at[pl.ds(0, rows)], sem).wait()


def _dispatch(dest, hf, n_rows):
    t = dest.shape[1]
    return pl.pallas_call(
        _dispatch_kernel,
        grid=(t // DMA_TOKENS,),
        in_specs=[pl.BlockSpec((TOP_K, DMA_TOKENS), lambda i: (0, i), memory_space=pltpu.SMEM),
                  pl.BlockSpec((DMA_TOKENS * TOKEN_ROWS, LANES), lambda i: (i, 0)),
                  pl.BlockSpec(memory_space=pl.ANY)],
        out_specs=pl.BlockSpec(memory_space=pl.ANY),
        scratch_shapes=[pltpu.SemaphoreType.DMA(())],
        out_shape=jax.ShapeDtypeStruct((n_rows * TOKEN_ROWS, LANES), jnp.float32),
        input_output_aliases={2: 0},
        compiler_params=pltpu.CompilerParams(
            dimension_semantics=("arbitrary",), vmem_limit_bytes=VMEM_LIMIT),
        name="dispatch_rows",
    )(dest, hf, jnp.zeros((n_rows * TOKEN_ROWS, LANES), jnp.float32))


def _expert_kernel(be_ref, nu_ref, xs_ref, w1_ref, w3_ref, w2_ref, y_ref):
    i = pl.program_id(0)

    @pl.when(i < nu_ref[0])
    def _():
        xb = _load_token_tiles(xs_ref, ROW_BLOCK).astype(jnp.bfloat16)
        h1 = jnp.dot(xb, w1_ref[0], preferred_element_type=jnp.float32)
        h3 = jnp.dot(xb, w3_ref[0], preferred_element_type=jnp.float32)
        act = (h1 * _sigmoid(h1) * h3).astype(jnp.bfloat16)
        _store_token_tiles(
            y_ref, jnp.dot(act, w2_ref[0], preferred_element_type=jnp.float32))

    @pl.when(i >= nu_ref[0])
    def _():
        y_ref[...] = jnp.zeros_like(y_ref)


def _experts(block_e, n_used, xs, w1, w3, w2):
    n_rows = xs.shape[0] // TOKEN_ROWS
    rows = lambda i, be, nu: (i, 0)
    wsel = lambda i, be, nu: (be[i], 0, 0)
    return pl.pallas_call(
        _expert_kernel,
        grid_spec=pltpu.PrefetchScalarGridSpec(
            num_scalar_prefetch=2,
            grid=(n_rows // ROW_BLOCK,),
            in_specs=[pl.BlockSpec((ROW_BLOCK * TOKEN_ROWS, LANES), rows),
                      pl.BlockSpec((1, D_MODEL, D_EXPERT), wsel),
                      pl.BlockSpec((1, D_MODEL, D_EXPERT), wsel),
                      pl.BlockSpec((1, D_EXPERT, D_MODEL), wsel)],
            out_specs=pl.BlockSpec((ROW_BLOCK * TOKEN_ROWS, LANES), rows)),
        out_shape=jax.ShapeDtypeStruct((n_rows * TOKEN_ROWS, LANES), jnp.float32),
        compiler_params=pltpu.CompilerParams(
            dimension_semantics=("arbitrary",), vmem_limit_bytes=VMEM_LIMIT),
        name="expert_mlp",
    )(block_e, n_used, xs, w1, w3, w2)


def _tail_kernel(dcur_ref, dnext_ref, x1_ref, y_hbm, gate_ref, p_ref, wple_ref, gproj_ref,
                 gple_ref, wpg_ref, gfin_ref, o_ref, ybuf_ref, sem, *, final_norm):
    tt = TAIL_TILE
    i = pl.program_id(0)
    slot = lax.rem(i, 2)

    def gather(d_ref, s):
        def issue(t, carry):
            for k in range(TOP_K):
                pltpu.make_async_copy(_token_tile(y_hbm, d_ref[k, t]),
                                      _token_tile(ybuf_ref.at[s], k * tt + t),
                                      sem.at[s]).start()
            return carry
        lax.fori_loop(0, tt, issue, 0, unroll=DMA_UNROLL)

    @pl.when(i == 0)
    def _():
        gather(dcur_ref, slot)

    @pl.when(i + 1 < pl.num_programs(0))
    def _():
        gather(dnext_ref, 1 - slot)

    pltpu.make_async_copy(y_hbm.at[pl.ds(0, TOP_K * tt * TOKEN_ROWS)], ybuf_ref.at[slot],
                          sem.at[slot]).wait()

    gpad = jnp.concatenate(
        [gate_ref[...], jnp.zeros((LANES - SUBLANES, tt), jnp.float32)], axis=0)
    gt = gpad.T
    yb = ybuf_ref.at[slot]
    x2 = (x1_ref[...] + gt[:, 0:1] * _load_token_tiles(yb, tt)
          + gt[:, 1:2] * _load_token_tiles(yb, tt, row0=tt * TOKEN_ROWS))
    e = _rms(jnp.dot(p_ref[...].astype(jnp.bfloat16), wple_ref[...],
                     preferred_element_type=jnp.float32), gproj_ref[...])
    hg = _rms(x2, gple_ref[...]).astype(jnp.bfloat16)
    pg = _sigmoid(jnp.dot(hg, wpg_ref[...], preferred_element_type=jnp.float32))
    x3 = x2 + pg * e
    o_ref[...] = _rms(x3, gfin_ref[...]) if final_norm else x3


def _tail(dest, x1, y, gate, p2d, wple, gproj, gple, wpg, gfin, final_norm):
    t = x1.shape[0]
    n_tiles = t // TAIL_TILE
    tok = lambda i: (i, 0)
    const = lambda i: (0, 0)

    def full(a):
        return pl.BlockSpec(a.shape, const)

    return pl.pallas_call(
        functools.partial(_tail_kernel, final_norm=final_norm),
        grid=(n_tiles,),
        in_specs=[pl.BlockSpec((TOP_K, TAIL_TILE), lambda i: (0, i), memory_space=pltpu.SMEM),
                  pl.BlockSpec((TOP_K, TAIL_TILE),
                               lambda i: (0, jnp.minimum(i + 1, n_tiles - 1)),
                               memory_space=pltpu.SMEM),
                  pl.BlockSpec((TAIL_TILE, D_MODEL), tok),
                  pl.BlockSpec(memory_space=pl.ANY),
                  pl.BlockSpec((SUBLANES, TAIL_TILE), lambda i: (0, i)),
                  pl.BlockSpec((TAIL_TILE, PLE_DIM), tok),
                  full(wple), full(gproj), full(gple), full(wpg), full(gfin)],
        out_specs=pl.BlockSpec((TAIL_TILE, D_MODEL), tok),
        out_shape=jax.ShapeDtypeStruct((t, D_MODEL), jnp.float32),
        scratch_shapes=[pltpu.VMEM((2, TOP_K * TAIL_TILE * TOKEN_ROWS, LANES), jnp.float32),
                        pltpu.SemaphoreType.DMA((2,))],
        compiler_params=pltpu.CompilerParams(
            dimension_semantics=("arbitrary",), vmem_limit_bytes=VMEM_LIMIT),
        name="tail",
    )(dest, dest, x1, y, gate, p2d, wple, gproj, gple, wpg, gfin)


def _block_diag(w):
    h, hd, _ = w.shape
    eye = jnp.eye(h, dtype=w.dtype)
    return jnp.einsum('hij,hg->higj', w, eye).reshape(h * hd, h * hd)


def _router_weights(w_group, b_group, w_expert, b_expert):
    w = jnp.zeros((D_MODEL, ROUTER_COLS), jnp.float32)
    w = w.at[:, 0:N_GROUPS].set(w_group)
    w = w.at[:, EXPERT_COL0:EXPERT_COL0 + N_EXPERTS].set(w_expert)
    bias = jnp.zeros((1, ROUTER_COLS), jnp.float32)
    bias = bias.at[0, 0:N_GROUPS].set(b_group)
    bias = bias.at[0, EXPERT_COL0:EXPERT_COL0 + N_EXPERTS].set(b_expert)
    w_hi = w.astype(jnp.bfloat16)
    w_lo = (w - w_hi.astype(jnp.float32)).astype(jnp.bfloat16)
    return w_hi, w_lo, bias


def _layer(x2d, p2d, batch, seq, final_norm, g_final, g_mix, w_in, conv_dw_w, conv_dw_b,
           conv_ln_g, conv_ln_b, lru_conv_w, lru_conv_b, lru_w_r, lru_b_r, lru_w_i, lru_b_i,
           lru_lambda, w_out, g_ffn, w_group, b_group, w_expert, b_expert, w1, w3, w2,
           g_ple, w_ple, g_ple_proj, w_ple_gate):
    bf16 = jnp.bfloat16
    t = batch * seq
    row = lambda v: v.reshape(1, -1)
    wg = jnp.concatenate([_block_diag(lru_w_r), _block_diag(lru_w_i)], axis=1).astype(bf16)
    bg = jnp.concatenate([lru_b_r, lru_b_i]).reshape(1, -1)
    wrh, wrl, br = _router_weights(w_group, b_group, w_expert, b_expert)
    ti = jnp.arange(SEQ_TILE)
    tri = (ti[:, None] < ti[None, :]).astype(bf16)

    x1, hf, eid, gate, rank, cnt = _mixer(
        x2d, batch, seq, row(g_mix), w_in.astype(bf16), conv_dw_w, row(conv_dw_b),
        row(conv_ln_g), row(conv_ln_b), lru_conv_w, row(lru_conv_b), wg, bg,
        row(lru_lambda), w_out.astype(bf16), row(g_ffn), wrh, wrl, br, tri)

    counts = cnt[:, 0].astype(jnp.int32)
    padded = (counts + ROW_BLOCK - 1) // ROW_BLOCK * ROW_BLOCK
    pad_end = jnp.cumsum(padded)
    pad_start = (pad_end - padded).astype(jnp.int32)
    n_blocks = (t * TOP_K) // ROW_BLOCK + N_EXPERTS
    n_rows = n_blocks * ROW_BLOCK
    block_start = jnp.arange(n_blocks, dtype=jnp.int32) * ROW_BLOCK
    block_e = jnp.sum((pad_end[None, :] <= block_start[:, None]).astype(jnp.int32), axis=1)
    block_e = jnp.minimum(block_e, N_EXPERTS - 1)
    n_used = (pad_end[-1:] // ROW_BLOCK).astype(jnp.int32)

    dest = _dest_rows(pad_start, eid, rank)
    xs = _dispatch(dest, hf, n_rows)
    y = _experts(block_e, n_used, xs, w1.astype(bf16), w3.astype(bf16), w2.astype(bf16))
    return _tail(dest, x1, y, gate, p2d, w_ple.astype(bf16), row(g_ple_proj), row(g_ple),
                 w_ple_gate.astype(bf16), row(g_final), final_norm)


def kernel(x, p, g_mix, w_in, conv_dw_w, conv_dw_b, conv_ln_g, conv_ln_b, lru_conv_w,
           lru_conv_b, lru_w_r, lru_b_r, lru_w_i, lru_b_i, lru_lambda, w_out, g_ffn,
           w_group, b_group, w_expert, b_expert, w1, w3, w2, g_ple, w_ple, g_ple_proj,
           w_ple_gate, g_final):
    batch, seq, d = x.shape
    assert d == D_MODEL and seq % SEQ_TILE == 0
    assert (batch * seq) % max(DMA_TOKENS, DEST_TOKENS, TAIL_TILE, ROW_BLOCK) == 0
    depth = w_in.shape[0]
    layers = (g_mix, w_in, conv_dw_w, conv_dw_b, conv_ln_g, conv_ln_b, lru_conv_w,
              lru_conv_b, lru_w_r, lru_b_r, lru_w_i, lru_b_i, lru_lambda, w_out, g_ffn,
              w_group, b_group, w_expert, b_expert, w1, w3, w2, g_ple, w_ple, g_ple_proj,
              w_ple_gate)
    x2d = x.reshape(batch * seq, d)
    for i in range(depth):
        x2d = _layer(x2d, p[i].reshape(batch * seq, PLE_DIM), batch, seq, i == depth - 1,
                     g_final, *(w[i] for w in layers))
    return x2d.reshape(batch, seq, d)
```

```python
import functools

import jax
import jax.numpy as jnp
from jax import lax
from jax.experimental import pallas as pl
from jax.experimental.pallas import tpu as pltpu

D_MODEL = 1024
D_CONV = 512
D_LRU = 512
LRU_HEADS = 8
LRU_HEAD_DIM = D_LRU // LRU_HEADS
CONV_WIDTH = 31
LRU_CONV_WIDTH = 4
LRU_C = 8.0
N_GROUPS = 4
EXPERTS_PER_GROUP = 8
N_EXPERTS = N_GROUPS * EXPERTS_PER_GROUP
TOP_K = 2
D_EXPERT = D_MODEL // 2
PLE_DIM = 256
EPS = 1e-6

SUBLANES = 8
LANES = 128

SEQ_TILE = 512
CONV_HIST = 32
LRU_HIST = SUBLANES
CONV_ROWS = 64
ROW_BLOCK = 512
DMA_TOKENS = 1024
DMA_UNROLL = 8
DEST_TOKENS = 4096
TAIL_TILE = 512
TAIL_STAGES = 4
ROUTER_COLS = LANES
EXPERT_COL0 = SUBLANES
VMEM_LIMIT = 56 * 1024 * 1024


def _sigmoid(v):
    return 1.0 / (1.0 + jnp.exp(-v))


def _rms(v, g):
    return v * lax.rsqrt(jnp.mean(v * v, axis=-1, keepdims=True) + EPS) * g


TOKEN_ROWS = D_MODEL // LANES


def _store_token_tiles(ref, v, row0=0):
    n = v.shape[0]
    for c in range(TOKEN_ROWS):
        ref[pl.ds(row0 + c, n, stride=TOKEN_ROWS), :] = v[:, c * LANES:(c + 1) * LANES]


def _load_token_tiles(ref, n, row0=0):
    return jnp.concatenate(
        [ref[pl.ds(row0 + c, n, stride=TOKEN_ROWS), :] for c in range(TOKEN_ROWS)], axis=1)


def _mixer_kernel(x_ref, gmix_ref, win_ref, cw_ref, cb_ref, lng_ref, lnb_ref,
                  lw_ref, lb_ref, wg_ref, bg_ref, lam_ref, wout_ref, gffn_ref,
                  wrh_ref, wrl_ref, br_ref, tri_ref,
                  x1_ref, hf_ref, eid_ref, gate_ref, rank_ref, cnt_ref,
                  uh_ref, ush_ref, lxh_ref, cv_ref, a_ref, h_ref, gl_ref, hcar_ref,
                  mix_ref, base_ref):
    ts = SEQ_TILE
    b = pl.program_id(0)
    s = pl.program_id(1)

    @pl.when(s == 0)
    def _():
        uh_ref[0:CONV_HIST, :] = jnp.zeros((CONV_HIST, D_CONV), jnp.float32)
        lxh_ref[0:LRU_HIST, :] = jnp.zeros((LRU_HIST, D_LRU), jnp.float32)
        hcar_ref[...] = jnp.zeros_like(hcar_ref)

    @pl.when((b == 0) & (s == 0))
    def _():
        base_ref[...] = jnp.zeros_like(base_ref)

    x = x_ref[...]
    hb = _rms(x, gmix_ref[...]).astype(jnp.bfloat16)

    def zcols(j):
        return jnp.dot(hb, win_ref[:, j * D_CONV:(j + 1) * D_CONV],
                       preferred_element_type=jnp.float32)

    uh_ref[CONV_HIST:CONV_HIST + ts, :] = zcols(0) * _sigmoid(zcols(1))
    lxh_ref[LRU_HIST:LRU_HIST + ts, :] = zcols(2)
    gl_ref[...] = jax.nn.gelu(zcols(3), approximate=True)

    span = ts + CONV_HIST - SUBLANES
    for r in range(SUBLANES):
        ush_ref[r] = uh_ref[SUBLANES - r:SUBLANES - r + span, :]
    for c in range(ts // CONV_ROWS):
        r0 = c * CONV_ROWS
        for lc in range(D_CONV // LANES):
            ls = slice(lc * LANES, (lc + 1) * LANES)
            acc = jnp.broadcast_to(cb_ref[:, ls], (CONV_ROWS, LANES))
            for j in range(CONV_WIDTH):
                k = CONV_WIDTH - 1 - j
                off = r0 + CONV_HIST - SUBLANES - SUBLANES * (j // SUBLANES)
                acc = acc + (ush_ref[j % SUBLANES, off:off + CONV_ROWS, ls]
                             * cw_ref[k:k + 1, ls])
            cv_ref[r0:r0 + CONV_ROWS, ls] = acc

    cv = cv_ref[...]
    mu = jnp.mean(cv, axis=-1, keepdims=True)
    dv = cv - mu
    var = jnp.mean(dv * dv, axis=-1, keepdims=True)
    yc = dv * lax.rsqrt(var + EPS) * lng_ref[...] + lnb_ref[...]
    mix_ref[:, 0:D_CONV] = (yc * _sigmoid(yc)).astype(jnp.bfloat16)

    xr = jnp.broadcast_to(lb_ref[...], (ts, D_LRU))
    for j in range(LRU_CONV_WIDTH):
        k = LRU_CONV_WIDTH - 1 - j
        xr = xr + lxh_ref[LRU_HIST - j:LRU_HIST - j + ts, :] * lw_ref[k:k + 1, :]
    gates = jnp.dot(xr.astype(jnp.bfloat16), wg_ref[...],
                    preferred_element_type=jnp.float32) + bg_ref[...]
    r_g = _sigmoid(gates[:, 0:D_LRU])
    i_g = _sigmoid(gates[:, D_LRU:2 * D_LRU])
    nlam = -lam_ref[...]
    softplus = jnp.maximum(nlam, 0.0) + jnp.log1p(jnp.exp(-jnp.abs(nlam)))
    log_a = -LRU_C * r_g * softplus
    a_ref[...] = jnp.exp(log_a)
    th = jnp.tanh(log_a)
    h_ref[...] = jnp.sqrt(-2.0 * th / (1.0 - th)) * (i_g * xr)

    row = lax.broadcasted_iota(jnp.int32, (SUBLANES, D_LRU), 0)

    def scan_group(g, hprev):
        r0 = pl.multiple_of(g * SUBLANES, SUBLANES)
        a = a_ref[pl.ds(r0, SUBLANES), :]
        hh = h_ref[pl.ds(r0, SUBLANES), :]
        for sh in (1, 2, 4):
            keep = row >= sh
            a_sh = jnp.where(keep, pltpu.roll(a, sh, 0), 1.0)
            h_sh = jnp.where(keep, pltpu.roll(hh, sh, 0), 0.0)
            hh = a * h_sh + hh
            a = a * a_sh
        hh = hh + a * hprev
        h_ref[pl.ds(r0, SUBLANES), :] = hh
        return hh[SUBLANES - 1:SUBLANES, :]

    hcar_ref[...] = lax.fori_loop(0, ts // SUBLANES, scan_group, hcar_ref[...])
    mix_ref[:, D_CONV:D_CONV + D_LRU] = (h_ref[...] * gl_ref[...]).astype(jnp.bfloat16)

    uh_ref[0:CONV_HIST, :] = uh_ref[ts:ts + CONV_HIST, :]
    lxh_ref[0:LRU_HIST, :] = lxh_ref[ts:ts + LRU_HIST, :]

    x1 = x + jnp.dot(mix_ref[...], wout_ref[...], preferred_element_type=jnp.float32)
    x1_ref[...] = x1
    hf = _rms(x1, gffn_ref[...])
    _store_token_tiles(hf_ref, hf)

    hf_hi = hf.astype(jnp.bfloat16)
    hf_lo = (hf - hf_hi.astype(jnp.float32)).astype(jnp.bfloat16)
    logits = (jnp.dot(hf_hi, wrh_ref[...], preferred_element_type=jnp.float32)
              + jnp.dot(hf_hi, wrl_ref[...], preferred_element_type=jnp.float32)
              + jnp.dot(hf_lo, wrh_ref[...], preferred_element_type=jnp.float32)
              + br_ref[...])
    lt = logits.T

    gl = [lt[g:g + 1, :] for g in range(N_GROUPS)]
    gmax = functools.reduce(jnp.maximum, gl)
    gidx = jnp.full((1, ts), N_GROUPS - 1, jnp.int32)
    for g in range(N_GROUPS - 2, -1, -1):
        gidx = jnp.where(gl[g] == gmax, g, gidx)
    gsum = functools.reduce(lambda p, q: p + q, [jnp.exp(v - gmax) for v in gl])
    gprob = 1.0 / gsum

    def eblock(g):
        lo = EXPERT_COL0 + g * EXPERTS_PER_GROUP
        return lt[lo:lo + EXPERTS_PER_GROUP, :]

    esel = eblock(N_GROUPS - 1)
    for g in range(N_GROUPS - 2, -1, -1):
        esel = jnp.where(gidx == g, eblock(g), esel)
    ridx = lax.broadcasted_iota(jnp.int32, (EXPERTS_PER_GROUP, ts), 0)
    m1 = jnp.max(esel, axis=0, keepdims=True)
    i1 = jnp.min(jnp.where(esel == m1, ridx, EXPERTS_PER_GROUP), axis=0, keepdims=True)
    rest = jnp.where(ridx == i1, -jnp.inf, esel)
    m2 = jnp.max(rest, axis=0, keepdims=True)
    i2 = jnp.min(jnp.where(rest == m2, ridx, EXPERTS_PER_GROUP), axis=0, keepdims=True)
    p2 = jnp.exp(m2 - m1)
    den = 1.0 + p2
    gate_ref[...] = jnp.concatenate(
        [gprob / den, gprob * p2 / den,
         jnp.zeros((SUBLANES - TOP_K, ts), jnp.float32)], axis=0)
    eid0 = gidx * EXPERTS_PER_GROUP + i1
    eid1 = gidx * EXPERTS_PER_GROUP + i2
    eid_ref[...] = jnp.concatenate([eid0, eid1], axis=0)

    eio = lax.broadcasted_iota(jnp.int32, (N_EXPERTS, ts), 0)
    oh0 = jnp.where(eio == eid0, 1.0, 0.0)
    oh1 = jnp.where(eio == eid1, 1.0, 0.0)
    tri = tri_ref[...]
    pre0 = jnp.dot(oh0.astype(jnp.bfloat16), tri, preferred_element_type=jnp.float32)
    pre1 = jnp.dot(oh1.astype(jnp.bfloat16), tri, preferred_element_type=jnp.float32)
    c0 = jnp.sum(oh0, axis=1, keepdims=True)
    c1 = jnp.sum(oh1, axis=1, keepdims=True)
    base = base_ref[:, 0:1]
    rank0 = jnp.sum(oh0 * (base + pre0), axis=0, keepdims=True)
    rank1 = jnp.sum(oh1 * (base + c0 + pre1), axis=0, keepdims=True)
    rank_ref[...] = jnp.concatenate([rank0, rank1], axis=0).astype(jnp.int32)
    total = jnp.broadcast_to(base + c0 + c1, (N_EXPERTS, LANES))
    base_ref[...] = total
    cnt_ref[...] = total


def _mixer(x2d, batch, seq, gmix, win, cw, cb, lng, lnb, lw, lb, wg, bg, lam, wout,
           gffn, wrh, wrl, br, tri):
    t = batch * seq
    ns = seq // SEQ_TILE
    tok = lambda b, s: (b * ns + s, 0)
    tokl = lambda b, s: (0, b * ns + s)
    const = lambda b, s: (0, 0)

    def full(a):
        return pl.BlockSpec(a.shape, const)

    weights = (gmix, win, cw, cb, lng, lnb, lw, lb, wg, bg, lam, wout, gffn, wrh, wrl, br, tri)
    return pl.pallas_call(
        _mixer_kernel,
        grid=(batch, ns),
        in_specs=[pl.BlockSpec((SEQ_TILE, D_MODEL), tok)] + [full(a) for a in weights],
        out_specs=[
            pl.BlockSpec((SEQ_TILE, D_MODEL), tok),
            pl.BlockSpec((SEQ_TILE * TOKEN_ROWS, LANES), tok),
            pl.BlockSpec((TOP_K, SEQ_TILE), tokl),
            pl.BlockSpec((SUBLANES, SEQ_TILE), tokl),
            pl.BlockSpec((TOP_K, SEQ_TILE), tokl),
            pl.BlockSpec((N_EXPERTS, LANES), const),
        ],
        out_shape=[
            jax.ShapeDtypeStruct((t, D_MODEL), jnp.float32),
            jax.ShapeDtypeStruct((t * TOKEN_ROWS, LANES), jnp.float32),
            jax.ShapeDtypeStruct((TOP_K, t), jnp.int32),
            jax.ShapeDtypeStruct((SUBLANES, t), jnp.float32),
            jax.ShapeDtypeStruct((TOP_K, t), jnp.int32),
            jax.ShapeDtypeStruct((N_EXPERTS, LANES), jnp.float32),
        ],
        scratch_shapes=[
            pltpu.VMEM((SEQ_TILE + CONV_HIST, D_CONV), jnp.float32),
            pltpu.VMEM((SUBLANES, SEQ_TILE + CONV_HIST - SUBLANES, D_CONV),
                       jnp.float32),
            pltpu.VMEM((SEQ_TILE + LRU_HIST, D_LRU), jnp.float32),
            pltpu.VMEM((SEQ_TILE, D_CONV), jnp.float32),
            pltpu.VMEM((SEQ_TILE, D_LRU), jnp.float32),
            pltpu.VMEM((SEQ_TILE, D_LRU), jnp.float32),
            pltpu.VMEM((SEQ_TILE, D_LRU), jnp.float32),
            pltpu.VMEM((1, D_LRU), jnp.float32),
            pltpu.VMEM((SEQ_TILE, D_MODEL), jnp.bfloat16),
            pltpu.VMEM((N_EXPERTS, LANES), jnp.float32),
        ],
        compiler_params=pltpu.CompilerParams(
            dimension_semantics=("arbitrary", "arbitrary"),
            vmem_limit_bytes=VMEM_LIMIT),
        name="mixer_router",
    )(x2d, *weights)


def _dest_kernel(ps_ref, eid_ref, rank_ref, dest_ref):
    eid = eid_ref[...]
    dest = rank_ref[...]
    for e in range(N_EXPERTS):
        dest = dest + jnp.where(eid == e, ps_ref[e], 0)
    dest_ref[...] = dest


def _dest_rows(pad_start, eid, rank):
    t = eid.shape[1]
    blk = pl.BlockSpec((TOP_K, DEST_TOKENS), lambda i, ps: (0, i))
    return pl.pallas_call(
        _dest_kernel,
        grid_spec=pltpu.PrefetchScalarGridSpec(
            num_scalar_prefetch=1, grid=(t // DEST_TOKENS,),
            in_specs=[blk, blk], out_specs=blk),
        out_shape=jax.ShapeDtypeStruct((TOP_K, t), jnp.int32),
        compiler_params=pltpu.CompilerParams(dimension_semantics=("arbitrary",)),
        name="dest_rows",
    )(pad_start, eid, rank)


def _token_tile(ref, tile):
    start = tile * TOKEN_ROWS
    if not isinstance(tile, int):
        start = pl.multiple_of(start, TOKEN_ROWS)
    return ref.at[pl.ds(start, TOKEN_ROWS), :]


def _dispatch_kernel(zlo_ref, zhi_ref, nu_ref, dest_ref, hf_ref, xs_hbm, zblk_ref, sem, zsem,
                     bsem):
    @pl.when(pl.program_id(0) == 0)
    def _():
        zblk_ref[...] = jnp.zeros_like(zblk_ref)
        ztile_ref = zblk_ref.at[pl.ds(0, TOKEN_ROWS), :]
        block_rows = ROW_BLOCK * TOKEN_ROWS
        n_blocks = xs_hbm.shape[0] // block_rows

        def zero_block(b):
            start = pl.multiple_of(b * block_rows, block_rows)
            return pltpu.make_async_copy(zblk_ref, xs_hbm.at[pl.ds(start, block_rows), :], bsem)

        def issue_block(b, carry):
            zero_block(b).start()
            return carry
        lax.fori_loop(nu_ref[0], n_blocks, issue_block, 0)

        def drain_block(b, carry):
            zero_block(b).wait()
            return carry
        lax.fori_loop(nu_ref[0], n_blocks, drain_block, 0)

        def zero_copy(r):
            return pltpu.make_async_copy(ztile_ref, _token_tile(xs_hbm, r), zsem)

        def per_expert(e, total):
            def issue_zero(r, carry):
                zero_copy(r).start()
                return carry
            lax.fori_loop(zlo_ref[e], zhi_ref[e], issue_zero, 0)
            return total + (zhi_ref[e] - zlo_ref[e])

        total = lax.fori_loop(0, N_EXPERTS, per_expert, 0)

        def drain(r, carry):
            zero_copy(0).wait()
            return carry
        lax.fori_loop(0, total, drain, 0)

    def issue(t, carry):
        for k in range(TOP_K):
            pltpu.make_async_copy(_token_tile(hf_ref, t),
                                  _token_tile(xs_hbm, dest_ref[k, t]), sem).start(priority=k)
        return carry

    lax.fori_loop(0, DMA_TOKENS, issue, 0, unroll=DMA_UNROLL)
    rows = TOP_K * DMA_TOKENS * TOKEN_ROWS
    pltpu.make_async_copy(xs_hbm.at[pl.ds(0, rows)], xs_hbm.at[pl.ds(0, rows)], sem).wait()


def _dispatch(zero_lo, zero_hi, n_used, dest, hf, n_rows):
    t = dest.shape[1]
    return pl.pallas_call(
        _dispatch_kernel,
        grid_spec=pltpu.PrefetchScalarGridSpec(
            num_scalar_prefetch=3,
            grid=(t // DMA_TOKENS,),
            in_specs=[pl.BlockSpec((TOP_K, DMA_TOKENS), lambda i, lo, hi, nu: (0, i),
                                   memory_space=pltpu.SMEM),
                      pl.BlockSpec((DMA_TOKENS * TOKEN_ROWS, LANES),
                                   lambda i, lo, hi, nu: (i, 0))],
            out_specs=pl.BlockSpec(memory_space=pl.ANY),
            scratch_shapes=[pltpu.VMEM((ROW_BLOCK * TOKEN_ROWS, LANES), jnp.float32),
                            pltpu.SemaphoreType.DMA(()),
                            pltpu.SemaphoreType.DMA(()),
                            pltpu.SemaphoreType.DMA(())]),
        out_shape=jax.ShapeDtypeStruct((n_rows * TOKEN_ROWS, LANES), jnp.float32),
        compiler_params=pltpu.CompilerParams(
            dimension_semantics=("arbitrary",), vmem_limit_bytes=VMEM_LIMIT),
        name="dispatch_rows",
    )(zero_lo, zero_hi, n_used, dest, hf)


def _expert_kernel(be_ref, nu_ref, xs_ref, w1_ref, w3_ref, w2_ref, y_ref,
                   w1b_ref, w3b_ref, w2b_ref):
    i = pl.program_id(0)
    used = i < nu_ref[0]

    @pl.when(used & ((i == 0) | (be_ref[i] != be_ref[jnp.maximum(i - 1, 0)])))
    def _():
        w1b_ref[...] = w1_ref[0].astype(jnp.bfloat16)
        w3b_ref[...] = w3_ref[0].astype(jnp.bfloat16)
        w2b_ref[...] = w2_ref[0].astype(jnp.bfloat16)

    @pl.when(used)
    def _():
        xb = _load_token_tiles(xs_ref, ROW_BLOCK).astype(jnp.bfloat16)
        h1 = jnp.dot(xb, w1b_ref[...], preferred_element_type=jnp.float32)
        h3 = jnp.dot(xb, w3b_ref[...], preferred_element_type=jnp.float32)
        act = (h1 * _sigmoid(h1) * h3).astype(jnp.bfloat16)
        _store_token_tiles(
            y_ref, jnp.dot(act, w2b_ref[...], preferred_element_type=jnp.float32))

    @pl.when(i >= nu_ref[0])
    def _():
        y_ref[...] = jnp.zeros_like(y_ref)


def _experts(block_e, n_used, xs, w1, w3, w2):
    n_rows = xs.shape[0] // TOKEN_ROWS
    rows = lambda i, be, nu: (i, 0)
    rows_in = lambda i, be, nu: (jnp.minimum(i, nu[0] - 1), 0)
    wsel = lambda i, be, nu: (be[i], 0, 0)
    return pl.pallas_call(
        _expert_kernel,
        grid_spec=pltpu.PrefetchScalarGridSpec(
            num_scalar_prefetch=2,
            grid=(n_rows // ROW_BLOCK,),
            in_specs=[pl.BlockSpec((ROW_BLOCK * TOKEN_ROWS, LANES), rows_in),
                      pl.BlockSpec((1, D_MODEL, D_EXPERT), wsel),
                      pl.BlockSpec((1, D_MODEL, D_EXPERT), wsel),
                      pl.BlockSpec((1, D_EXPERT, D_MODEL), wsel)],
            out_specs=pl.BlockSpec((ROW_BLOCK * TOKEN_ROWS, LANES), rows),
            scratch_shapes=[pltpu.VMEM((D_MODEL, D_EXPERT), jnp.bfloat16),
                            pltpu.VMEM((D_MODEL, D_EXPERT), jnp.bfloat16),
                            pltpu.VMEM((D_EXPERT, D_MODEL), jnp.bfloat16)]),
        out_shape=jax.ShapeDtypeStruct((n_rows * TOKEN_ROWS, LANES), jnp.float32),
        compiler_params=pltpu.CompilerParams(
            dimension_semantics=("arbitrary",), vmem_limit_bytes=VMEM_LIMIT),
        name="expert_mlp",
    )(block_e, n_used, xs, w1, w3, w2)


def _tail_kernel(dcur_ref, dnext_ref, x1_ref, y_hbm, gate_ref, p_ref, wple_ref, gproj_ref,
                 gple_ref, wpg_ref, gfin_ref, o_ref, ybuf_ref, sem, *, final_norm):
    tt = TAIL_TILE
    i = pl.program_id(0)

    def tile_copy(d_ref, s, k, t, t0):
        return pltpu.make_async_copy(_token_tile(y_hbm, d_ref[k, t0 + t]),
                                     _token_tile(ybuf_ref.at[s], k * tt + t), sem.at[s])

    def wait_slot(s):
        pltpu.make_async_copy(y_hbm.at[pl.ds(0, TOP_K * tt * TOKEN_ROWS)], ybuf_ref.at[s],
                              sem.at[s]).wait()

    def issue_inline(d_ref, s, t0, part):
        for t in range(part * tt // TAIL_STAGES, (part + 1) * tt // TAIL_STAGES):
            for k in range(TOP_K):
                tile_copy(d_ref, s, k, t, t0).start()

    def compute(s, r0, d_ref, s_next, t0_next):
        rows = slice(r0, r0 + tt)
        gpad = jnp.concatenate(
            [gate_ref[:, rows], jnp.zeros((LANES - SUBLANES, tt), jnp.float32)], axis=0)
        gt = gpad.T
        yb = ybuf_ref.at[s]
        issue_inline(d_ref, s_next, t0_next, 0)
        x2 = (x1_ref[rows, :] + gt[:, 0:1] * _load_token_tiles(yb, tt)
              + gt[:, 1:2] * _load_token_tiles(yb, tt, row0=tt * TOKEN_ROWS))
        issue_inline(d_ref, s_next, t0_next, 1)
        e = _rms(jnp.dot(p_ref[rows, :].astype(jnp.bfloat16), wple_ref[...],
                         preferred_element_type=jnp.float32), gproj_ref[...])
        hg = _rms(x2, gple_ref[...]).astype(jnp.bfloat16)
        issue_inline(d_ref, s_next, t0_next, 2)
        pg = _sigmoid(jnp.dot(hg, wpg_ref[...], preferred_element_type=jnp.float32))
        issue_inline(d_ref, s_next, t0_next, 3)
        x3 = x2 + pg * e
        o_ref[rows, :] = _rms(x3, gfin_ref[...]) if final_norm else x3

    @pl.when(i == 0)
    def _():
        def issue(t, carry):
            for k in range(TOP_K):
                tile_copy(dcur_ref, 0, k, t, 0).start()
            return carry
        lax.fori_loop(0, tt, issue, 0, unroll=DMA_UNROLL)

    wait_slot(0)
    compute(0, 0, dcur_ref, 1, tt)
    wait_slot(1)
    compute(1, tt, dnext_ref, 0, 0)

    @pl.when(i == pl.num_programs(0) - 1)
    def _():
        wait_slot(0)


def _tail(dest, x1, y, gate, p2d, wple, gproj, gple, wpg, gfin, final_norm):
    t = x1.shape[0]
    step = 2 * TAIL_TILE
    n_steps = t // step
    tok = lambda i: (i, 0)
    const = lambda i: (0, 0)

    def full(a):
        return pl.BlockSpec(a.shape, const)

    return pl.pallas_call(
        functools.partial(_tail_kernel, final_norm=final_norm),
        grid=(n_steps,),
        in_specs=[pl.BlockSpec((TOP_K, step), lambda i: (0, i), memory_space=pltpu.SMEM),
                  pl.BlockSpec((TOP_K, step), lambda i: (0, jnp.minimum(i + 1, n_steps - 1)),
                               memory_space=pltpu.SMEM),
                  pl.BlockSpec((step, D_MODEL), tok),
                  pl.BlockSpec(memory_space=pl.ANY),
                  pl.BlockSpec((SUBLANES, step), lambda i: (0, i)),
                  pl.BlockSpec((step, PLE_DIM), tok),
                  full(wple), full(gproj), full(gple), full(wpg), full(gfin)],
        out_specs=pl.BlockSpec((step, D_MODEL), tok),
        out_shape=jax.ShapeDtypeStruct((t, D_MODEL), jnp.float32),
        scratch_shapes=[pltpu.VMEM((2, TOP_K * TAIL_TILE * TOKEN_ROWS, LANES), jnp.float32),
                        pltpu.SemaphoreType.DMA((2,))],
        compiler_params=pltpu.CompilerParams(
            dimension_semantics=("arbitrary",), vmem_limit_bytes=VMEM_LIMIT),
        name="tail",
    )(dest, dest, x1, y, gate, p2d, wple, gproj, gple, wpg, gfin)


def _block_diag(w):
    h, hd, _ = w.shape
    eye = jnp.eye(h, dtype=w.dtype)
    return jnp.einsum('hij,hg->higj', w, eye).reshape(h * hd, h * hd)


def _router_weights(w_group, b_group, w_expert, b_expert):
    w = jnp.zeros((D_MODEL, ROUTER_COLS), jnp.float32)
    w = w.at[:, 0:N_GROUPS].set(w_group)
    w = w.at[:, EXPERT_COL0:EXPERT_COL0 + N_EXPERTS].set(w_expert)
    bias = jnp.zeros((1, ROUTER_COLS), jnp.float32)
    bias = bias.at[0, 0:N_GROUPS].set(b_group)
    bias = bias.at[0, EXPERT_COL0:EXPERT_COL0 + N_EXPERTS].set(b_expert)
    w_hi = w.astype(jnp.bfloat16)
    w_lo = (w - w_hi.astype(jnp.float32)).astype(jnp.bfloat16)
    return w_hi, w_lo, bias


def _layer(x2d, p2d, batch, seq, final_norm, g_final, g_mix, w_in, conv_dw_w, conv_dw_b,
           conv_ln_g, conv_ln_b, lru_conv_w, lru_conv_b, lru_w_r, lru_b_r, lru_w_i, lru_b_i,
           lru_lambda, w_out, g_ffn, w_group, b_group, w_expert, b_expert, w1, w3, w2,
           g_ple, w_ple, g_ple_proj, w_ple_gate):
    bf16 = jnp.bfloat16
    t = batch * seq
    row = lambda v: v.reshape(1, -1)
    wg = jnp.concatenate([_block_diag(lru_w_r), _block_diag(lru_w_i)], axis=1).astype(bf16)
    bg = jnp.concatenate([lru_b_r, lru_b_i]).reshape(1, -1)
    wrh, wrl, br = _router_weights(w_group, b_group, w_expert, b_expert)
    ti = jnp.arange(SEQ_TILE)
    tri = (ti[:, None] < ti[None, :]).astype(bf16)

    x1, hf, eid, gate, rank, cnt = _mixer(
        x2d, batch, seq, row(g_mix), w_in.astype(bf16), conv_dw_w, row(conv_dw_b),
        row(conv_ln_g), row(conv_ln_b), lru_conv_w, row(lru_conv_b), wg, bg,
        row(lru_lambda), w_out.astype(bf16), row(g_ffn), wrh, wrl, br, tri)

    counts = cnt[:, 0].astype(jnp.int32)
    padded = (counts + ROW_BLOCK - 1) // ROW_BLOCK * ROW_BLOCK
    pad_end = jnp.cumsum(padded)
    pad_start = (pad_end - padded).astype(jnp.int32)
    n_blocks = (t * TOP_K) // ROW_BLOCK + N_EXPERTS
    n_rows = n_blocks * ROW_BLOCK
    block_start = jnp.arange(n_blocks, dtype=jnp.int32) * ROW_BLOCK
    block_e = jnp.sum((pad_end[None, :] <= block_start[:, None]).astype(jnp.int32), axis=1)
    block_e = jnp.minimum(block_e, N_EXPERTS - 1)
    n_used = (pad_end[-1:] // ROW_BLOCK).astype(jnp.int32)

    dest = _dest_rows(pad_start, eid, rank)
    xs = _dispatch(pad_start + counts, pad_end.astype(jnp.int32), n_used, dest, hf, n_rows)
    y = _experts(block_e, n_used, xs, w1, w3, w2)
    return _tail(dest, x1, y, gate, p2d, w_ple.astype(bf16), row(g_ple_proj), row(g_ple),
                 w_ple_gate.astype(bf16), row(g_final), final_norm)


def kernel(x, p, g_mix, w_in, conv_dw_w, conv_dw_b, conv_ln_g, conv_ln_b, lru_conv_w,
           lru_conv_b, lru_w_r, lru_b_r, lru_w_i, lru_b_i, lru_lambda, w_out, g_ffn,
           w_group, b_group, w_expert, b_expert, w1, w3, w2, g_ple, w_ple, g_ple_proj,
           w_ple_gate, g_final):
    batch, seq, d = x.shape
    assert d == D_MODEL and seq % SEQ_TILE == 0
    assert (batch * seq) % max(DMA_TOKENS, DEST_TOKENS, 2 * TAIL_TILE, ROW_BLOCK) == 0
    depth = w_in.shape[0]
    layers = (g_mix, w_in, conv_dw_w, conv_dw_b, conv_ln_g, conv_ln_b, lru_conv_w,
              lru_conv_b, lru_w_r, lru_b_r, lru_w_i, lru_b_i, lru_lambda, w_out, g_ffn,
              w_group, b_group, w_expert, b_expert, w1, w3, w2, g_ple, w_ple, g_ple_proj,
              w_ple_gate)
    x2d = x.reshape(batch * seq, d)
    for i in range(depth):
        x2d = _layer(x2d, p[i].reshape(batch * seq, PLE_DIM), batch, seq, i == depth - 1,
                     g_final, *(w[i] for w in layers))
    return x2d.reshape(batch, seq, d)
```

```python
import functools

import jax
import jax.numpy as jnp
from jax import lax
from jax.experimental import pallas as pl
from jax.experimental.pallas import tpu as pltpu

D_MODEL = 1024
D_CONV = 512
D_LRU = 512
LRU_HEADS = 8
LRU_HEAD_DIM = D_LRU // LRU_HEADS
CONV_WIDTH = 31
LRU_CONV_WIDTH = 4
LRU_C = 8.0
N_GROUPS = 4
EXPERTS_PER_GROUP = 8
N_EXPERTS = N_GROUPS * EXPERTS_PER_GROUP
TOP_K = 2
D_EXPERT = D_MODEL // 2
PLE_DIM = 256
EPS = 1e-6

SUBLANES = 8
LANES = 128

SEQ_TILE = 512
CONV_HIST = 32
LRU_HIST = SUBLANES
CONV_ROWS = 64
ROW_BLOCK = 512
DMA_UNROLL = 8
TAIL_TILE = 512
TAIL_STAGES = 4
ROUTER_COLS = LANES
EXPERT_COL0 = SUBLANES
VMEM_LIMIT = 56 * 1024 * 1024


def _sigmoid(v):
    return 1.0 / (1.0 + jnp.exp(-v))


def _rms(v, g):
    return v * lax.rsqrt(jnp.mean(v * v, axis=-1, keepdims=True) + EPS) * g


TOKEN_ROWS = D_MODEL // LANES


def _store_token_tiles(ref, v, row0=0):
    n = v.shape[0]
    for c in range(TOKEN_ROWS):
        ref[pl.ds(row0 + c, n, stride=TOKEN_ROWS), :] = v[:, c * LANES:(c + 1) * LANES]


def _load_token_tiles(ref, n, row0=0):
    return jnp.concatenate(
        [ref[pl.ds(row0 + c, n, stride=TOKEN_ROWS), :] for c in range(TOKEN_ROWS)], axis=1)


def _token_tile(ref, tile):
    start = tile * TOKEN_ROWS
    if not isinstance(tile, int):
        start = pl.multiple_of(start, TOKEN_ROWS)
    return ref.at[pl.ds(start, TOKEN_ROWS), :]


def _mixer_kernel(x_ref, gmix_ref, win_ref, cw_ref, cb_ref, lng_ref, lnb_ref,
                  lw_ref, lb_ref, wg_ref, bg_ref, lam_ref, wout_ref, gffn_ref,
                  wrc_ref, wrh_ref, br_ref, tri_ref, tril_ref,
                  x1_ref, gate_ref, dest_ref, be_ref, nu_ref, xs_hbm,
                  uh_ref, ush_ref, lxh_ref, cv_ref, a_ref, h_ref, gl_ref, hcar_ref,
                  mix_ref, tot_ref, cur_ref, nal_ref, bet_ref, hfbuf_ref, dvec_ref,
                  dsm_ref, zvec_ref, zsm_ref, dsem, ssem, zsem, bsem, isem):
    ts = SEQ_TILE
    b = pl.program_id(0)
    s = pl.program_id(1)
    step = b * pl.num_programs(1) + s
    last = pl.num_programs(0) * pl.num_programs(1) - 1
    slot = lax.rem(step, 2)
    prev = 1 - slot
    n_blocks = xs_hbm.shape[0] // (ROW_BLOCK * TOKEN_ROWS)
    inv_block = 1.0 / ROW_BLOCK

    def dest_copy(sl):
        return pltpu.make_async_copy(dvec_ref.at[sl], dsm_ref.at[sl], dsem.at[sl])

    def scatter_copy(sl, k, t):
        return pltpu.make_async_copy(_token_tile(hfbuf_ref.at[sl], t),
                                     _token_tile(xs_hbm, dsm_ref[sl, k, t]), ssem)

    def scatter_wait():
        rows = TOP_K * ts * TOKEN_ROWS
        pltpu.make_async_copy(xs_hbm.at[pl.ds(0, rows)], xs_hbm.at[pl.ds(0, rows)], ssem).wait()

    @pl.when(s == 0)
    def _():
        uh_ref[0:CONV_HIST, :] = jnp.zeros((CONV_HIST, D_CONV), jnp.float32)
        lxh_ref[0:LRU_HIST, :] = jnp.zeros((LRU_HIST, D_LRU), jnp.float32)
        hcar_ref[...] = jnp.zeros_like(hcar_ref)

    @pl.when(step == 0)
    def _():
        tot_ref[...] = jnp.zeros_like(tot_ref)
        cur_ref[...] = jnp.zeros_like(cur_ref)
        nal_ref[...] = jnp.zeros_like(nal_ref)
        bet_ref[...] = jnp.zeros_like(bet_ref)
        hfbuf_ref[1] = jnp.zeros((ts * TOKEN_ROWS, LANES), jnp.float32)

        def init(t, carry):
            for k in range(TOP_K):
                dsm_ref[1, k, t] = k * ts + t
            return carry
        lax.fori_loop(0, ts, init, 0)

    @pl.when(step > 0)
    def _():
        dest_copy(prev).wait()

    x = x_ref[...]
    hb = _rms(x, gmix_ref[...]).astype(jnp.bfloat16)

    def zcols(j):
        return jnp.dot(hb, win_ref[:, j * D_CONV:(j + 1) * D_CONV],
                       preferred_element_type=jnp.float32)

    uh_ref[CONV_HIST:CONV_HIST + ts, :] = zcols(0) * _sigmoid(zcols(1))
    lxh_ref[LRU_HIST:LRU_HIST + ts, :] = zcols(2)
    gl_ref[...] = jax.nn.gelu(zcols(3), approximate=True)

    span = ts + CONV_HIST - SUBLANES
    for r in range(SUBLANES):
        ush_ref[r] = uh_ref[SUBLANES - r:SUBLANES - r + span, :]
    n_chunks = ts // CONV_ROWS
    for c in range(n_chunks):
        r0 = c * CONV_ROWS
        for lc in range(D_CONV // LANES):
            ls = slice(lc * LANES, (lc + 1) * LANES)
            acc = jnp.broadcast_to(cb_ref[:, ls], (CONV_ROWS, LANES))
            for j in range(CONV_WIDTH):
                k = CONV_WIDTH - 1 - j
                off = r0 + CONV_HIST - SUBLANES - SUBLANES * (j // SUBLANES)
                acc = acc + (ush_ref[j % SUBLANES, off:off + CONV_ROWS, ls]
                             * cw_ref[k:k + 1, ls])
            cv_ref[r0:r0 + CONV_ROWS, ls] = acc
        for t in range(c * ts // n_chunks, (c + 1) * ts // n_chunks):
            for k in range(TOP_K):
                scatter_copy(prev, k, t).start(priority=k)

    cv = cv_ref[...]
    mu = jnp.mean(cv, axis=-1, keepdims=True)
    dv = cv - mu
    var = jnp.mean(dv * dv, axis=-1, keepdims=True)
    yc = dv * lax.rsqrt(var + EPS) * lng_ref[...] + lnb_ref[...]
    mix_ref[:, 0:D_CONV] = (yc * _sigmoid(yc)).astype(jnp.bfloat16)

    xr = jnp.broadcast_to(lb_ref[...], (ts, D_LRU))
    for j in range(LRU_CONV_WIDTH):
        k = LRU_CONV_WIDTH - 1 - j
        xr = xr + lxh_ref[LRU_HIST - j:LRU_HIST - j + ts, :] * lw_ref[k:k + 1, :]
    gates = jnp.dot(xr.astype(jnp.bfloat16), wg_ref[...],
                    preferred_element_type=jnp.float32) + bg_ref[...]
    r_g = _sigmoid(gates[:, 0:D_LRU])
    i_g = _sigmoid(gates[:, D_LRU:2 * D_LRU])
    nlam = -lam_ref[...]
    softplus = jnp.maximum(nlam, 0.0) + jnp.log1p(jnp.exp(-jnp.abs(nlam)))
    log_a = -LRU_C * r_g * softplus
    a_ref[...] = jnp.exp(log_a)
    th = jnp.tanh(log_a)
    h_ref[...] = jnp.sqrt(-2.0 * th / (1.0 - th)) * (i_g * xr)

    row = lax.broadcasted_iota(jnp.int32, (SUBLANES, D_LRU), 0)

    def scan_group(g, hprev):
        r0 = pl.multiple_of(g * SUBLANES, SUBLANES)
        a = a_ref[pl.ds(r0, SUBLANES), :]
        hh = h_ref[pl.ds(r0, SUBLANES), :]
        for sh in (1, 2, 4):
            keep = row >= sh
            a_sh = jnp.where(keep, pltpu.roll(a, sh, 0), 1.0)
            h_sh = jnp.where(keep, pltpu.roll(hh, sh, 0), 0.0)
            hh = a * h_sh + hh
            a = a * a_sh
        hh = hh + a * hprev
        h_ref[pl.ds(r0, SUBLANES), :] = hh
        return hh[SUBLANES - 1:SUBLANES, :]

    hcar_ref[...] = lax.fori_loop(0, ts // SUBLANES, scan_group, hcar_ref[...])
    mix_ref[:, D_CONV:D_CONV + D_LRU] = (h_ref[...] * gl_ref[...]).astype(jnp.bfloat16)

    uh_ref[0:CONV_HIST, :] = uh_ref[ts:ts + CONV_HIST, :]
    lxh_ref[0:LRU_HIST, :] = lxh_ref[ts:ts + LRU_HIST, :]

    x1 = x + jnp.dot(mix_ref[...], wout_ref[...], preferred_element_type=jnp.float32)
    x1_ref[...] = x1
    hf = _rms(x1, gffn_ref[...])
    _store_token_tiles(hfbuf_ref.at[slot], hf)

    hf_hi = hf.astype(jnp.bfloat16)
    hf_lo = (hf - hf_hi.astype(jnp.float32)).astype(jnp.bfloat16)
    hi_both = jnp.dot(hf_hi, wrc_ref[...], preferred_element_type=jnp.float32)
    logits = (hi_both[:, 0:ROUTER_COLS] + hi_both[:, ROUTER_COLS:2 * ROUTER_COLS]
              + jnp.dot(hf_lo, wrh_ref[...], preferred_element_type=jnp.float32)
              + br_ref[...])
    lt = logits.T

    gl = [lt[g:g + 1, :] for g in range(N_GROUPS)]
    gmax = functools.reduce(jnp.maximum, gl)
    gidx = jnp.full((1, ts), N_GROUPS - 1, jnp.int32)
    for g in range(N_GROUPS - 2, -1, -1):
        gidx = jnp.where(gl[g] == gmax, g, gidx)
    gsum = functools.reduce(lambda p, q: p + q, [jnp.exp(v - gmax) for v in gl])
    gprob = 1.0 / gsum

    def eblock(g):
        lo = EXPERT_COL0 + g * EXPERTS_PER_GROUP
        return lt[lo:lo + EXPERTS_PER_GROUP, :]

    esel = eblock(N_GROUPS - 1)
    for g in range(N_GROUPS - 2, -1, -1):
        esel = jnp.where(gidx == g, eblock(g), esel)
    ridx = lax.broadcasted_iota(jnp.int32, (EXPERTS_PER_GROUP, ts), 0)
    m1 = jnp.max(esel, axis=0, keepdims=True)
    i1 = jnp.min(jnp.where(esel == m1, ridx, EXPERTS_PER_GROUP), axis=0, keepdims=True)
    rest = jnp.where(ridx == i1, -jnp.inf, esel)
    m2 = jnp.max(rest, axis=0, keepdims=True)
    i2 = jnp.min(jnp.where(rest == m2, ridx, EXPERTS_PER_GROUP), axis=0, keepdims=True)
    p2 = jnp.exp(m2 - m1)
    den = 1.0 + p2
    gate_ref[...] = jnp.concatenate(
        [gprob / den, gprob * p2 / den,
         jnp.zeros((SUBLANES - TOP_K, ts), jnp.float32)], axis=0)
    eid0 = gidx * EXPERTS_PER_GROUP + i1
    eid1 = gidx * EXPERTS_PER_GROUP + i2

    eio = lax.broadcasted_iota(jnp.int32, (N_EXPERTS, ts), 0)
    oh0 = jnp.where(eio == eid0, 1.0, 0.0)
    oh1 = jnp.where(eio == eid1, 1.0, 0.0)
    tri = tri_ref[...]
    pre0 = jnp.dot(oh0.astype(jnp.bfloat16), tri, preferred_element_type=jnp.float32)
    pre1 = jnp.dot(oh1.astype(jnp.bfloat16), tri, preferred_element_type=jnp.float32)
    c0 = jnp.sum(oh0, axis=1, keepdims=True)
    c1 = jnp.sum(oh1, axis=1, keepdims=True)
    before = tot_ref[:, 0:1]
    rank0 = jnp.sum(oh0 * (before + pre0), axis=0, keepdims=True)
    rank1 = jnp.sum(oh1 * (before + c0 + pre1), axis=0, keepdims=True)

    after = before + c0 + c1
    nbb = jnp.floor((before + (ROW_BLOCK - 1)) * inv_block)
    nba = jnp.floor((after + (ROW_BLOCK - 1)) * inv_block)
    new = nba - nbb
    new_wide = jnp.broadcast_to(new, (N_EXPERTS, LANES))
    newbase = nal_ref[:, 0:1] + jnp.dot(tril_ref[...], new_wide.astype(jnp.bfloat16),
                                         preferred_element_type=jnp.float32)[:, 0:1]
    curb = cur_ref[:, 0:1]
    voff = newbase - nbb

    def place(oh, p):
        def look(col):
            return jnp.sum(oh * col, axis=0, keepdims=True)
        q = jnp.floor(p * inv_block)
        blk = jnp.where(q < look(nbb), look(curb), look(voff) + q)
        return blk * ROW_BLOCK + (p - q * ROW_BLOCK)

    dest = jnp.concatenate([place(oh0, rank0), place(oh1, rank1)], axis=0).astype(jnp.int32)
    dest_ref[...] = dest
    dvec_ref[slot] = dest
    dest_copy(slot).start()

    tot_ref[...] = jnp.broadcast_to(after, (N_EXPERTS, LANES))
    cur_ref[...] = jnp.broadcast_to(jnp.where(new > 0, newbase + new - 1.0, curb),
                                    (N_EXPERTS, LANES))
    nal_new = nal_ref[...] + jnp.sum(new_wide, axis=0, keepdims=True)
    nal_ref[...] = nal_new
    nb_pad = bet_ref.shape[1]
    jcol = lax.broadcasted_iota(jnp.int32, (N_EXPERTS, nb_pad), 1).astype(jnp.float32)
    erow = lax.broadcasted_iota(jnp.int32, (N_EXPERTS, nb_pad), 0).astype(jnp.float32)
    mine = jnp.where(jcol >= newbase, 1.0, 0.0) * jnp.where(jcol < newbase + new, 1.0, 0.0)
    owner = jnp.sum(mine * erow, axis=0, keepdims=True)
    taken = jnp.sum(mine, axis=0, keepdims=True)
    bet = jnp.where(taken > 0.0, owner, bet_ref[0:1, :])
    bet_ref[...] = jnp.broadcast_to(bet, bet_ref.shape)
    be_ref[...] = jnp.broadcast_to(bet, bet_ref.shape).astype(jnp.int32)
    nu_ref[...] = nal_new[0:SUBLANES, :].astype(jnp.int32)

    scatter_wait()

    @pl.when(step == last)
    def _():
        dest_copy(slot).wait()

        def issue(t, carry):
            for k in range(TOP_K):
                scatter_copy(slot, k, t).start(priority=k)
            return carry
        lax.fori_loop(0, ts, issue, 0, unroll=DMA_UNROLL)
        scatter_wait()

        fill = after - ROW_BLOCK * jnp.floor(after * inv_block)
        zvec_ref[0] = cur_ref[...].astype(jnp.int32)
        zvec_ref[1] = jnp.broadcast_to(fill, (N_EXPERTS, LANES)).astype(jnp.int32)
        zvec_ref[2] = nal_new.astype(jnp.int32)
        info = pltpu.make_async_copy(zvec_ref, zsm_ref, isem)
        info.start()
        zsrc = hfbuf_ref.at[prev]
        zsrc[...] = jnp.zeros((ts * TOKEN_ROWS, LANES), jnp.float32)
        info.wait()

        def zero_tile(r):
            return pltpu.make_async_copy(_token_tile(zsrc, 0), _token_tile(xs_hbm, r), zsem)

        def per_expert(e, total):
            filled = zsm_ref[1, e, 0]
            lo = jnp.where(filled > 0, filled, ROW_BLOCK)
            row0 = zsm_ref[0, e, 0] * ROW_BLOCK

            def issue_zero(r, carry):
                zero_tile(row0 + r).start()
                return carry
            lax.fori_loop(lo, ROW_BLOCK, issue_zero, 0)
            return total + (ROW_BLOCK - lo)
        total = lax.fori_loop(0, N_EXPERTS, per_expert, 0)

        def drain(r, carry):
            zero_tile(0).wait()
            return carry
        lax.fori_loop(0, total, drain, 0)

        block_rows = ROW_BLOCK * TOKEN_ROWS
        assert block_rows == ts * TOKEN_ROWS

        def zero_block(blk):
            start = pl.multiple_of(blk * block_rows, block_rows)
            return pltpu.make_async_copy(zsrc, xs_hbm.at[pl.ds(start, block_rows), :], bsem)

        def issue_block(blk, carry):
            zero_block(blk).start()
            return carry
        lax.fori_loop(zsm_ref[2, 0, 0], n_blocks, issue_block, 0)

        def drain_block(blk, carry):
            zero_block(blk).wait()
            return carry
        lax.fori_loop(zsm_ref[2, 0, 0], n_blocks, drain_block, 0)


def _mixer(x2d, batch, seq, n_rows, gmix, win, cw, cb, lng, lnb, lw, lb, wg, bg, lam, wout,
           gffn, wrc, wrh, br, tri, tril):
    t = batch * seq
    ns = seq // SEQ_TILE
    nb_pad = -(-(n_rows // ROW_BLOCK) // LANES) * LANES
    tok = lambda b, s: (b * ns + s, 0)
    tokl = lambda b, s: (0, b * ns + s)
    const = lambda b, s: (0, 0)

    def full(a):
        return pl.BlockSpec(a.shape, const)

    weights = (gmix, win, cw, cb, lng, lnb, lw, lb, wg, bg, lam, wout, gffn, wrc, wrh, br,
               tri, tril)
    f32, i32 = jnp.float32, jnp.int32
    return pl.pallas_call(
        _mixer_kernel,
        grid=(batch, ns),
        in_specs=[pl.BlockSpec((SEQ_TILE, D_MODEL), tok)] + [full(a) for a in weights],
        out_specs=[
            pl.BlockSpec((SEQ_TILE, D_MODEL), tok),
            pl.BlockSpec((SUBLANES, SEQ_TILE), tokl),
            pl.BlockSpec((TOP_K, SEQ_TILE), tokl),
            pl.BlockSpec((SUBLANES, nb_pad), const),
            pl.BlockSpec((SUBLANES, LANES), const),
            pl.BlockSpec(memory_space=pl.ANY),
        ],
        out_shape=[
            jax.ShapeDtypeStruct((t, D_MODEL), f32),
            jax.ShapeDtypeStruct((SUBLANES, t), f32),
            jax.ShapeDtypeStruct((TOP_K, t), i32),
            jax.ShapeDtypeStruct((SUBLANES, nb_pad), i32),
            jax.ShapeDtypeStruct((SUBLANES, LANES), i32),
            jax.ShapeDtypeStruct((n_rows * TOKEN_ROWS, LANES), f32),
        ],
        scratch_shapes=[
            pltpu.VMEM((SEQ_TILE + CONV_HIST, D_CONV), f32),
            pltpu.VMEM((SUBLANES, SEQ_TILE + CONV_HIST - SUBLANES, D_CONV), f32),
            pltpu.VMEM((SEQ_TILE + LRU_HIST, D_LRU), f32),
            pltpu.VMEM((SEQ_TILE, D_CONV), f32),
            pltpu.VMEM((SEQ_TILE, D_LRU), f32),
            pltpu.VMEM((SEQ_TILE, D_LRU), f32),
            pltpu.VMEM((SEQ_TILE, D_LRU), f32),
            pltpu.VMEM((1, D_LRU), f32),
            pltpu.VMEM((SEQ_TILE, D_MODEL), jnp.bfloat16),
            pltpu.VMEM((N_EXPERTS, LANES), f32),
            pltpu.VMEM((N_EXPERTS, LANES), f32),
            pltpu.VMEM((N_EXPERTS, LANES), f32),
            pltpu.VMEM((SUBLANES, nb_pad), f32),
            pltpu.VMEM((2, SEQ_TILE * TOKEN_ROWS, LANES), f32),
            pltpu.VMEM((2, TOP_K, SEQ_TILE), i32),
            pltpu.SMEM((2, TOP_K, SEQ_TILE), i32),
            pltpu.VMEM((3, N_EXPERTS, LANES), i32),
            pltpu.SMEM((3, N_EXPERTS, LANES), i32),
            pltpu.SemaphoreType.DMA((2,)),
            pltpu.SemaphoreType.DMA(()),
            pltpu.SemaphoreType.DMA(()),
            pltpu.SemaphoreType.DMA(()),
            pltpu.SemaphoreType.DMA(()),
        ],
        compiler_params=pltpu.CompilerParams(
            dimension_semantics=("arbitrary", "arbitrary"),
            vmem_limit_bytes=VMEM_LIMIT),
        name="mixer_router",
    )(x2d, *weights)


def _expert_kernel(be_ref, nu_ref, xs_ref, w1_ref, w3_ref, w2_ref, y_ref,
                   w1b_ref, w3b_ref, w2b_ref):
    i = pl.program_id(0)
    used = i < nu_ref[0]

    @pl.when(used & ((i == 0) | (be_ref[i] != be_ref[jnp.maximum(i - 1, 0)])))
    def _():
        w1b_ref[...] = w1_ref[0].astype(jnp.bfloat16)
        w3b_ref[...] = w3_ref[0].astype(jnp.bfloat16)
        w2b_ref[...] = w2_ref[0].astype(jnp.bfloat16)

    @pl.when(used)
    def _():
        xb = _load_token_tiles(xs_ref, ROW_BLOCK).astype(jnp.bfloat16)
        h1 = jnp.dot(xb, w1b_ref[...], preferred_element_type=jnp.float32)
        h3 = jnp.dot(xb, w3b_ref[...], preferred_element_type=jnp.float32)
        act = (h1 * _sigmoid(h1) * h3).astype(jnp.bfloat16)
        _store_token_tiles(
            y_ref, jnp.dot(act, w2b_ref[...], preferred_element_type=jnp.float32))

    @pl.when(i >= nu_ref[0])
    def _():
        y_ref[...] = jnp.zeros_like(y_ref)


def _experts(block_e, n_used, xs, w1, w3, w2):
    n_rows = xs.shape[0] // TOKEN_ROWS
    rows = lambda i, be, nu: (i, 0)
    rows_in = lambda i, be, nu: (jnp.minimum(i, nu[0] - 1), 0)
    wsel = lambda i, be, nu: (be[i], 0, 0)
    return pl.pallas_call(
        _expert_kernel,
        grid_spec=pltpu.PrefetchScalarGridSpec(
            num_scalar_prefetch=2,
            grid=(n_rows // ROW_BLOCK,),
            in_specs=[pl.BlockSpec((ROW_BLOCK * TOKEN_ROWS, LANES), rows_in),
                      pl.BlockSpec((1, D_MODEL, D_EXPERT), wsel),
                      pl.BlockSpec((1, D_MODEL, D_EXPERT), wsel),
                      pl.BlockSpec((1, D_EXPERT, D_MODEL), wsel)],
            out_specs=pl.BlockSpec((ROW_BLOCK * TOKEN_ROWS, LANES), rows),
            scratch_shapes=[pltpu.VMEM((D_MODEL, D_EXPERT), jnp.bfloat16),
                            pltpu.VMEM((D_MODEL, D_EXPERT), jnp.bfloat16),
                            pltpu.VMEM((D_EXPERT, D_MODEL), jnp.bfloat16)]),
        out_shape=jax.ShapeDtypeStruct((n_rows * TOKEN_ROWS, LANES), jnp.float32),
        compiler_params=pltpu.CompilerParams(
            dimension_semantics=("arbitrary",), vmem_limit_bytes=VMEM_LIMIT),
        name="expert_mlp",
    )(block_e, n_used, xs, w1, w3, w2)


def _tail_kernel(dcur_ref, dnext_ref, x1_ref, y_hbm, gate_ref, p_ref, wple_ref, gproj_ref,
                 gple_ref, wpg_ref, gfin_ref, o_ref, ybuf_ref, sem, *, final_norm):
    tt = TAIL_TILE
    i = pl.program_id(0)

    def tile_copy(d_ref, s, k, t, t0):
        return pltpu.make_async_copy(_token_tile(y_hbm, d_ref[k, t0 + t]),
                                     _token_tile(ybuf_ref.at[s], k * tt + t), sem.at[s])

    def wait_slot(s):
        pltpu.make_async_copy(y_hbm.at[pl.ds(0, TOP_K * tt * TOKEN_ROWS)], ybuf_ref.at[s],
                              sem.at[s]).wait()

    def issue_inline(d_ref, s, t0, part):
        for t in range(part * tt // TAIL_STAGES, (part + 1) * tt // TAIL_STAGES):
            for k in range(TOP_K):
                tile_copy(d_ref, s, k, t, t0).start(priority=k)

    def compute(s, r0, d_ref, s_next, t0_next):
        rows = slice(r0, r0 + tt)
        gpad = jnp.concatenate(
            [gate_ref[:, rows], jnp.zeros((LANES - SUBLANES, tt), jnp.float32)], axis=0)
        gt = gpad.T
        yb = ybuf_ref.at[s]
        issue_inline(d_ref, s_next, t0_next, 0)
        x2 = (x1_ref[rows, :] + gt[:, 0:1] * _load_token_tiles(yb, tt)
              + gt[:, 1:2] * _load_token_tiles(yb, tt, row0=tt * TOKEN_ROWS))
        issue_inline(d_ref, s_next, t0_next, 1)
        e = _rms(jnp.dot(p_ref[rows, :].astype(jnp.bfloat16), wple_ref[...],
                         preferred_element_type=jnp.float32), gproj_ref[...])
        hg = _rms(x2, gple_ref[...]).astype(jnp.bfloat16)
        issue_inline(d_ref, s_next, t0_next, 2)
        pg = _sigmoid(jnp.dot(hg, wpg_ref[...], preferred_element_type=jnp.float32))
        issue_inline(d_ref, s_next, t0_next, 3)
        x3 = x2 + pg * e
        o_ref[rows, :] = _rms(x3, gfin_ref[...]) if final_norm else x3

    @pl.when(i == 0)
    def _():
        def issue(t, carry):
            for k in range(TOP_K):
                tile_copy(dcur_ref, 0, k, t, 0).start()
            return carry
        lax.fori_loop(0, tt, issue, 0, unroll=DMA_UNROLL)

    wait_slot(0)
    compute(0, 0, dcur_ref, 1, tt)
    wait_slot(1)
    compute(1, tt, dnext_ref, 0, 0)

    @pl.when(i == pl.num_programs(0) - 1)
    def _():
        wait_slot(0)


def _tail(dest, x1, y, gate, p2d, wple, gproj, gple, wpg, gfin, final_norm):
    t = x1.shape[0]
    step = 2 * TAIL_TILE
    n_steps = t // step
    tok = lambda i: (i, 0)
    const = lambda i: (0, 0)

    def full(a):
        return pl.BlockSpec(a.shape, const)

    return pl.pallas_call(
        functools.partial(_tail_kernel, final_norm=final_norm),
        grid=(n_steps,),
        in_specs=[pl.BlockSpec((TOP_K, step), lambda i: (0, i), memory_space=pltpu.SMEM),
                  pl.BlockSpec((TOP_K, step), lambda i: (0, jnp.minimum(i + 1, n_steps - 1)),
                               memory_space=pltpu.SMEM),
                  pl.BlockSpec((step, D_MODEL), tok),
                  pl.BlockSpec(memory_space=pl.ANY),
                  pl.BlockSpec((SUBLANES, step), lambda i: (0, i)),
                  pl.BlockSpec((step, PLE_DIM), tok),
                  full(wple), full(gproj), full(gple), full(wpg), full(gfin)],
        out_specs=pl.BlockSpec((step, D_MODEL), tok),
        out_shape=jax.ShapeDtypeStruct((t, D_MODEL), jnp.float32),
        scratch_shapes=[pltpu.VMEM((2, TOP_K * TAIL_TILE * TOKEN_ROWS, LANES), jnp.float32),
                        pltpu.SemaphoreType.DMA((2,))],
        compiler_params=pltpu.CompilerParams(
            dimension_semantics=("arbitrary",), vmem_limit_bytes=VMEM_LIMIT),
        name="tail",
    )(dest, dest, x1, y, gate, p2d, wple, gproj, gple, wpg, gfin)


def _block_diag(w):
    h, hd, _ = w.shape
    eye = jnp.eye(h, dtype=w.dtype)
    return jnp.einsum('hij,hg->higj', w, eye).reshape(h * hd, h * hd)


def _router_weights(w_group, b_group, w_expert, b_expert):
    w = jnp.zeros((D_MODEL, ROUTER_COLS), jnp.float32)
    w = w.at[:, 0:N_GROUPS].set(w_group)
    w = w.at[:, EXPERT_COL0:EXPERT_COL0 + N_EXPERTS].set(w_expert)
    bias = jnp.zeros((1, ROUTER_COLS), jnp.float32)
    bias = bias.at[0, 0:N_GROUPS].set(b_group)
    bias = bias.at[0, EXPERT_COL0:EXPERT_COL0 + N_EXPERTS].set(b_expert)
    w_hi = w.astype(jnp.bfloat16)
    w_lo = (w - w_hi.astype(jnp.float32)).astype(jnp.bfloat16)
    return jnp.concatenate([w_hi, w_lo], axis=1), w_hi, bias


def _layer(x2d, p2d, batch, seq, final_norm, g_final, g_mix, w_in, conv_dw_w, conv_dw_b,
           conv_ln_g, conv_ln_b, lru_conv_w, lru_conv_b, lru_w_r, lru_b_r, lru_w_i, lru_b_i,
           lru_lambda, w_out, g_ffn, w_group, b_group, w_expert, b_expert, w1, w3, w2,
           g_ple, w_ple, g_ple_proj, w_ple_gate):
    bf16 = jnp.bfloat16
    t = batch * seq
    row = lambda v: v.reshape(1, -1)
    wg = jnp.concatenate([_block_diag(lru_w_r), _block_diag(lru_w_i)], axis=1).astype(bf16)
    bg = jnp.concatenate([lru_b_r, lru_b_i]).reshape(1, -1)
    wrc, wrh, br = _router_weights(w_group, b_group, w_expert, b_expert)
    ti = jnp.arange(SEQ_TILE)
    tri = (ti[:, None] < ti[None, :]).astype(bf16)
    ei = jnp.arange(N_EXPERTS)
    tril = (ei[None, :] < ei[:, None]).astype(bf16)
    n_blocks = (t * TOP_K) // ROW_BLOCK + N_EXPERTS
    n_rows = n_blocks * ROW_BLOCK

    x1, gate, dest, be, nu, xs = _mixer(
        x2d, batch, seq, n_rows, row(g_mix), w_in.astype(bf16), conv_dw_w, row(conv_dw_b),
        row(conv_ln_g), row(conv_ln_b), lru_conv_w, row(lru_conv_b), wg, bg,
        row(lru_lambda), w_out.astype(bf16), row(g_ffn), wrc, wrh, br, tri, tril)
    y = _experts(be[0, :n_blocks], nu[0, 0:1], xs, w1, w3, w2)
    return _tail(dest, x1, y, gate, p2d, w_ple.astype(bf16), row(g_ple_proj), row(g_ple),
                 w_ple_gate.astype(bf16), row(g_final), final_norm)


def kernel(x, p, g_mix, w_in, conv_dw_w, conv_dw_b, conv_ln_g, conv_ln_b, lru_conv_w,
           lru_conv_b, lru_w_r, lru_b_r, lru_w_i, lru_b_i, lru_lambda, w_out, g_ffn,
           w_group, b_group, w_expert, b_expert, w1, w3, w2, g_ple, w_ple, g_ple_proj,
           w_ple_gate, g_final):
    batch, seq, d = x.shape
    assert d == D_MODEL and seq % SEQ_TILE == 0 and SEQ_TILE == ROW_BLOCK
    assert (batch * seq) % max(2 * TAIL_TILE, ROW_BLOCK) == 0
    depth = w_in.shape[0]
    layers = (g_mix, w_in, conv_dw_w, conv_dw_b, conv_ln_g, conv_ln_b, lru_conv_w,
              lru_conv_b, lru_w_r, lru_b_r, lru_w_i, lru_b_i, lru_lambda, w_out, g_ffn,
              w_group, b_group, w_expert, b_expert, w1, w3, w2, g_ple, w_ple, g_ple_proj,
              w_ple_gate)
    x2d = x.reshape(batch * seq, d)
    for i in range(depth):
        x2d = _layer(x2d, p[i].reshape(batch * seq, PLE_DIM), batch, seq, i == depth - 1,
                     g_final, *(w[i] for w in layers))
    return x2d.reshape(batch, seq, d)
```

```python
import functools

import jax
import jax.numpy as jnp
from jax import lax
from jax.experimental import pallas as pl
from jax.experimental.pallas import tpu as pltpu

D_MODEL = 1024
D_CONV = 512
D_LRU = 512
LRU_HEADS = 8
LRU_HEAD_DIM = D_LRU // LRU_HEADS
CONV_WIDTH = 31
LRU_CONV_WIDTH = 4
LRU_C = 8.0
N_GROUPS = 4
EXPERTS_PER_GROUP = 8
N_EXPERTS = N_GROUPS * EXPERTS_PER_GROUP
TOP_K = 2
D_EXPERT = D_MODEL // 2
PLE_DIM = 256
EPS = 1e-6

SUBLANES = 8
LANES = 128

SEQ_TILE = 512
CONV_HIST = 32
LRU_HIST = SUBLANES
CONV_ROWS = 64
ROW_BLOCK = 512
DMA_UNROLL = 8
TAIL_TILE = 512
TAIL_STAGES = 4
ROUTER_COLS = LANES
EXPERT_COL0 = SUBLANES
VMEM_LIMIT = 56 * 1024 * 1024


def _sigmoid(v):
    return 1.0 / (1.0 + jnp.exp(-v))


def _rms(v, g):
    return v * lax.rsqrt(jnp.mean(v * v, axis=-1, keepdims=True) + EPS) * g


TOKEN_ROWS = D_MODEL // LANES


def _store_token_tiles(ref, v, row0=0):
    n = v.shape[0]
    for c in range(TOKEN_ROWS):
        ref[pl.ds(row0 + c, n, stride=TOKEN_ROWS), :] = v[:, c * LANES:(c + 1) * LANES]


def _load_token_tiles(ref, n, row0=0):
    return jnp.concatenate(
        [ref[pl.ds(row0 + c, n, stride=TOKEN_ROWS), :] for c in range(TOKEN_ROWS)], axis=1)


def _token_tile(ref, tile):
    start = tile * TOKEN_ROWS
    if not isinstance(tile, int):
        start = pl.multiple_of(start, TOKEN_ROWS)
    return ref.at[pl.ds(start, TOKEN_ROWS), :]


def _mixer_kernel(x_ref, gmix_ref, win_ref, cw_ref, cb_ref, lng_ref, lnb_ref,
                  lw_ref, lb_ref, wg_ref, bg_ref, lam_ref, wout_ref, gffn_ref,
                  wrc_ref, wrh_ref, br_ref, tri_ref, tril_ref,
                  x1_ref, gate_ref, dest_ref, be_ref, nu_ref, xs_hbm,
                  uh_ref, ush_ref, lxh_ref, cv_ref, a_ref, h_ref, gl_ref, hcar_ref,
                  mix_ref, tot_ref, cur_ref, nal_ref, bet_ref, hfbuf_ref, dvec_ref,
                  dsm_ref, zvec_ref, zsm_ref, dsem, ssem, zsem, bsem, isem):
    ts = SEQ_TILE
    b = pl.program_id(0)
    s = pl.program_id(1)
    step = b * pl.num_programs(1) + s
    last = pl.num_programs(0) * pl.num_programs(1) - 1
    slot = lax.rem(step, 2)
    prev = 1 - slot
    n_blocks = xs_hbm.shape[0] // (ROW_BLOCK * TOKEN_ROWS)
    inv_block = 1.0 / ROW_BLOCK

    def dest_copy(sl):
        return pltpu.make_async_copy(dvec_ref.at[sl], dsm_ref.at[sl], dsem.at[sl])

    def scatter_copy(sl, k, t):
        return pltpu.make_async_copy(_token_tile(hfbuf_ref.at[sl], t),
                                     _token_tile(xs_hbm, dsm_ref[sl, k, t]), ssem)

    def scatter_wait():
        rows = TOP_K * ts * TOKEN_ROWS
        pltpu.make_async_copy(xs_hbm.at[pl.ds(0, rows)], xs_hbm.at[pl.ds(0, rows)], ssem).wait()

    @pl.when(s == 0)
    def _():
        uh_ref[0:CONV_HIST, :] = jnp.zeros((CONV_HIST, D_CONV), jnp.float32)
        lxh_ref[0:LRU_HIST, :] = jnp.zeros((LRU_HIST, D_LRU), jnp.float32)
        hcar_ref[...] = jnp.zeros_like(hcar_ref)

    @pl.when(step == 0)
    def _():
        tot_ref[...] = jnp.zeros_like(tot_ref)
        cur_ref[...] = jnp.zeros_like(cur_ref)
        nal_ref[...] = jnp.zeros_like(nal_ref)
        bet_ref[...] = jnp.zeros_like(bet_ref)
        hfbuf_ref[1] = jnp.zeros((ts * TOKEN_ROWS, LANES), jnp.float32)

        def init(t, carry):
            for k in range(TOP_K):
                dsm_ref[1, k, t] = k * ts + t
            return carry
        lax.fori_loop(0, ts, init, 0)

    @pl.when(step > 0)
    def _():
        dest_copy(prev).wait()

    x = x_ref[...]
    hb = _rms(x, gmix_ref[...]).astype(jnp.bfloat16)

    def zcols(j):
        return jnp.dot(hb, win_ref[:, j * D_CONV:(j + 1) * D_CONV],
                       preferred_element_type=jnp.float32)

    uh_ref[CONV_HIST:CONV_HIST + ts, :] = zcols(0) * _sigmoid(zcols(1))
    lxh_ref[LRU_HIST:LRU_HIST + ts, :] = zcols(2)
    gl_ref[...] = jax.nn.gelu(zcols(3), approximate=True)

    span = ts + CONV_HIST - SUBLANES
    u_ext = uh_ref[...]
    for r in range(1, SUBLANES):
        ush_ref[r] = pltpu.roll(u_ext, r, 0)[SUBLANES:SUBLANES + span, :]
    n_chunks = ts // CONV_ROWS
    for c in range(n_chunks):
        r0 = c * CONV_ROWS
        for lc in range(D_CONV // LANES):
            ls = slice(lc * LANES, (lc + 1) * LANES)
            acc = jnp.broadcast_to(cb_ref[:, ls], (CONV_ROWS, LANES))
            for j in range(CONV_WIDTH):
                k = CONV_WIDTH - 1 - j
                off = r0 + CONV_HIST - SUBLANES - SUBLANES * (j // SUBLANES)
                if j % SUBLANES == 0:
                    u_j = uh_ref[off + SUBLANES:off + SUBLANES + CONV_ROWS, ls]
                else:
                    u_j = ush_ref[j % SUBLANES, off:off + CONV_ROWS, ls]
                acc = acc + u_j * cw_ref[k:k + 1, ls]
            cv_ref[r0:r0 + CONV_ROWS, ls] = acc
        for t in range(c * ts // n_chunks, (c + 1) * ts // n_chunks):
            for k in range(TOP_K):
                scatter_copy(prev, k, t).start(priority=k)

    cv = cv_ref[...]
    mu = jnp.mean(cv, axis=-1, keepdims=True)
    dv = cv - mu
    var = jnp.mean(dv * dv, axis=-1, keepdims=True)
    yc = dv * lax.rsqrt(var + EPS) * lng_ref[...] + lnb_ref[...]
    mix_ref[:, 0:D_CONV] = (yc * _sigmoid(yc)).astype(jnp.bfloat16)

    xr = jnp.broadcast_to(lb_ref[...], (ts, D_LRU))
    for j in range(LRU_CONV_WIDTH):
        k = LRU_CONV_WIDTH - 1 - j
        xr = xr + lxh_ref[LRU_HIST - j:LRU_HIST - j + ts, :] * lw_ref[k:k + 1, :]
    gates = jnp.dot(xr.astype(jnp.bfloat16), wg_ref[...],
                    preferred_element_type=jnp.float32) + bg_ref[...]
    r_g = _sigmoid(gates[:, 0:D_LRU])
    i_g = _sigmoid(gates[:, D_LRU:2 * D_LRU])
    nlam = -lam_ref[...]
    softplus = jnp.maximum(nlam, 0.0) + jnp.log1p(jnp.exp(-jnp.abs(nlam)))
    log_a = -LRU_C * r_g * softplus
    a_ref[...] = jnp.exp(log_a)
    th = jnp.tanh(log_a)
    h_ref[...] = jnp.sqrt(-2.0 * th / (1.0 - th)) * (i_g * xr)

    row = lax.broadcasted_iota(jnp.int32, (SUBLANES, D_LRU), 0)

    def scan_group(g, hprev):
        r0 = pl.multiple_of(g * SUBLANES, SUBLANES)
        a = a_ref[pl.ds(r0, SUBLANES), :]
        hh = h_ref[pl.ds(r0, SUBLANES), :]
        for sh in (1, 2, 4):
            keep = row >= sh
            a_sh = jnp.where(keep, pltpu.roll(a, sh, 0), 1.0)
            h_sh = jnp.where(keep, pltpu.roll(hh, sh, 0), 0.0)
            hh = a * h_sh + hh
            a = a * a_sh
        hh = hh + a * hprev
        h_ref[pl.ds(r0, SUBLANES), :] = hh
        return hh[SUBLANES - 1:SUBLANES, :]

    hcar_ref[...] = lax.fori_loop(0, ts // SUBLANES, scan_group, hcar_ref[...])
    mix_ref[:, D_CONV:D_CONV + D_LRU] = (h_ref[...] * gl_ref[...]).astype(jnp.bfloat16)

    uh_ref[0:CONV_HIST, :] = uh_ref[ts:ts + CONV_HIST, :]
    lxh_ref[0:LRU_HIST, :] = lxh_ref[ts:ts + LRU_HIST, :]

    x1 = x + jnp.dot(mix_ref[...], wout_ref[...], preferred_element_type=jnp.float32)
    x1_ref[...] = x1
    hf = _rms(x1, gffn_ref[...])
    _store_token_tiles(hfbuf_ref.at[slot], hf)

    hf_hi = hf.astype(jnp.bfloat16)
    hf_lo = (hf - hf_hi.astype(jnp.float32)).astype(jnp.bfloat16)
    hi_both = jnp.dot(hf_hi, wrc_ref[...], preferred_element_type=jnp.float32)
    logits = (hi_both[:, 0:ROUTER_COLS] + hi_both[:, ROUTER_COLS:2 * ROUTER_COLS]
              + jnp.dot(hf_lo, wrh_ref[...], preferred_element_type=jnp.float32)
              + br_ref[...])
    lt = logits.T

    gl = [lt[g:g + 1, :] for g in range(N_GROUPS)]
    gmax = functools.reduce(jnp.maximum, gl)
    gidx = jnp.full((1, ts), N_GROUPS - 1, jnp.int32)
    for g in range(N_GROUPS - 2, -1, -1):
        gidx = jnp.where(gl[g] == gmax, g, gidx)
    gsum = functools.reduce(lambda p, q: p + q, [jnp.exp(v - gmax) for v in gl])
    gprob = 1.0 / gsum

    def eblock(g):
        lo = EXPERT_COL0 + g * EXPERTS_PER_GROUP
        return lt[lo:lo + EXPERTS_PER_GROUP, :]

    esel = eblock(N_GROUPS - 1)
    for g in range(N_GROUPS - 2, -1, -1):
        esel = jnp.where(gidx == g, eblock(g), esel)
    ridx = lax.broadcasted_iota(jnp.int32, (EXPERTS_PER_GROUP, ts), 0)
    m1 = jnp.max(esel, axis=0, keepdims=True)
    i1 = jnp.min(jnp.where(esel == m1, ridx, EXPERTS_PER_GROUP), axis=0, keepdims=True)
    rest = jnp.where(ridx == i1, -jnp.inf, esel)
    m2 = jnp.max(rest, axis=0, keepdims=True)
    i2 = jnp.min(jnp.where(rest == m2, ridx, EXPERTS_PER_GROUP), axis=0, keepdims=True)
    p2 = jnp.exp(m2 - m1)
    den = 1.0 + p2
    gate_ref[...] = jnp.concatenate(
        [gprob / den, gprob * p2 / den,
         jnp.zeros((SUBLANES - TOP_K, ts), jnp.float32)], axis=0)
    eid0 = gidx * EXPERTS_PER_GROUP + i1
    eid1 = gidx * EXPERTS_PER_GROUP + i2

    eio = lax.broadcasted_iota(jnp.int32, (N_EXPERTS, ts), 0)
    oh0 = jnp.where(eio == eid0, 1.0, 0.0)
    oh1 = jnp.where(eio == eid1, 1.0, 0.0)
    tri = tri_ref[...]
    pre0 = jnp.dot(oh0.astype(jnp.bfloat16), tri, preferred_element_type=jnp.float32)
    pre1 = jnp.dot(oh1.astype(jnp.bfloat16), tri, preferred_element_type=jnp.float32)
    c0 = jnp.sum(oh0, axis=1, keepdims=True)
    c1 = jnp.sum(oh1, axis=1, keepdims=True)
    before = tot_ref[:, 0:1]
    rank0 = jnp.sum(oh0 * (before + pre0), axis=0, keepdims=True)
    rank1 = jnp.sum(oh1 * (before + c0 + pre1), axis=0, keepdims=True)

    after = before + c0 + c1
    nbb = jnp.floor((before + (ROW_BLOCK - 1)) * inv_block)
    nba = jnp.floor((after + (ROW_BLOCK - 1)) * inv_block)
    new = nba - nbb
    new_wide = jnp.broadcast_to(new, (N_EXPERTS, LANES))
    newbase = nal_ref[:, 0:1] + jnp.dot(tril_ref[...], new_wide.astype(jnp.bfloat16),
                                         preferred_element_type=jnp.float32)[:, 0:1]
    curb = cur_ref[:, 0:1]
    voff = newbase - nbb

    def place(oh, p):
        def look(col):
            return jnp.sum(oh * col, axis=0, keepdims=True)
        q = jnp.floor(p * inv_block)
        blk = jnp.where(q < look(nbb), look(curb), look(voff) + q)
        return blk * ROW_BLOCK + (p - q * ROW_BLOCK)

    dest = jnp.concatenate([place(oh0, rank0), place(oh1, rank1)], axis=0).astype(jnp.int32)
    dest_ref[...] = dest
    dvec_ref[slot] = dest
    dest_copy(slot).start()

    tot_ref[...] = jnp.broadcast_to(after, (N_EXPERTS, LANES))
    cur_ref[...] = jnp.broadcast_to(jnp.where(new > 0, newbase + new - 1.0, curb),
                                    (N_EXPERTS, LANES))
    nal_new = nal_ref[...] + jnp.sum(new_wide, axis=0, keepdims=True)
    nal_ref[...] = nal_new
    nb_pad = bet_ref.shape[1]
    jcol = lax.broadcasted_iota(jnp.int32, (N_EXPERTS, nb_pad), 1).astype(jnp.float32)
    erow = lax.broadcasted_iota(jnp.int32, (N_EXPERTS, nb_pad), 0).astype(jnp.float32)
    mine = jnp.where(jcol >= newbase, 1.0, 0.0) * jnp.where(jcol < newbase + new, 1.0, 0.0)
    owner = jnp.sum(mine * erow, axis=0, keepdims=True)
    taken = jnp.sum(mine, axis=0, keepdims=True)
    bet = jnp.where(taken > 0.0, owner, bet_ref[0:1, :])
    bet_ref[...] = jnp.broadcast_to(bet, bet_ref.shape)
    be_ref[...] = jnp.broadcast_to(bet, bet_ref.shape).astype(jnp.int32)
    nu_ref[...] = nal_new[0:SUBLANES, :].astype(jnp.int32)

    scatter_wait()

    @pl.when(step == last)
    def _():
        dest_copy(slot).wait()

        def issue(t, carry):
            for k in range(TOP_K):
                scatter_copy(slot, k, t).start(priority=k)
            return carry
        lax.fori_loop(0, ts, issue, 0, unroll=DMA_UNROLL)
        scatter_wait()

        fill = after - ROW_BLOCK * jnp.floor(after * inv_block)
        zvec_ref[0] = cur_ref[...].astype(jnp.int32)
        zvec_ref[1] = jnp.broadcast_to(fill, (N_EXPERTS, LANES)).astype(jnp.int32)
        zvec_ref[2] = nal_new.astype(jnp.int32)
        info = pltpu.make_async_copy(zvec_ref, zsm_ref, isem)
        info.start()
        zsrc = hfbuf_ref.at[prev]
        zsrc[...] = jnp.zeros((ts * TOKEN_ROWS, LANES), jnp.float32)
        info.wait()

        def zero_tile(r):
            return pltpu.make_async_copy(_token_tile(zsrc, 0), _token_tile(xs_hbm, r), zsem)

        def per_expert(e, total):
            filled = zsm_ref[1, e, 0]
            lo = jnp.where(filled > 0, filled, ROW_BLOCK)
            row0 = zsm_ref[0, e, 0] * ROW_BLOCK

            def issue_zero(r, carry):
                zero_tile(row0 + r).start()
                return carry
            lax.fori_loop(lo, ROW_BLOCK, issue_zero, 0)
            return total + (ROW_BLOCK - lo)
        total = lax.fori_loop(0, N_EXPERTS, per_expert, 0)

        def drain(r, carry):
            zero_tile(0).wait()
            return carry
        lax.fori_loop(0, total, drain, 0)

        block_rows = ROW_BLOCK * TOKEN_ROWS
        assert block_rows == ts * TOKEN_ROWS

        def zero_block(blk):
            start = pl.multiple_of(blk * block_rows, block_rows)
            return pltpu.make_async_copy(zsrc, xs_hbm.at[pl.ds(start, block_rows), :], bsem)

        def issue_block(blk, carry):
            zero_block(blk).start()
            return carry
        lax.fori_loop(zsm_ref[2, 0, 0], n_blocks, issue_block, 0)

        def drain_block(blk, carry):
            zero_block(blk).wait()
            return carry
        lax.fori_loop(zsm_ref[2, 0, 0], n_blocks, drain_block, 0)


def _mixer(x2d, batch, seq, n_rows, gmix, win, cw, cb, lng, lnb, lw, lb, wg, bg, lam, wout,
           gffn, wrc, wrh, br, tri, tril):
    t = batch * seq
    ns = seq // SEQ_TILE
    nb_pad = -(-(n_rows // ROW_BLOCK) // LANES) * LANES
    tok = lambda b, s: (b * ns + s, 0)
    tokl = lambda b, s: (0, b * ns + s)
    const = lambda b, s: (0, 0)

    def full(a):
        return pl.BlockSpec(a.shape, const)

    weights = (gmix, win, cw, cb, lng, lnb, lw, lb, wg, bg, lam, wout, gffn, wrc, wrh, br,
               tri, tril)
    f32, i32 = jnp.float32, jnp.int32
    return pl.pallas_call(
        _mixer_kernel,
        grid=(batch, ns),
        in_specs=[pl.BlockSpec((SEQ_TILE, D_MODEL), tok)] + [full(a) for a in weights],
        out_specs=[
            pl.BlockSpec((SEQ_TILE, D_MODEL), tok),
            pl.BlockSpec((SUBLANES, SEQ_TILE), tokl),
            pl.BlockSpec((TOP_K, SEQ_TILE), tokl),
            pl.BlockSpec((SUBLANES, nb_pad), const),
            pl.BlockSpec((SUBLANES, LANES), const),
            pl.BlockSpec(memory_space=pl.ANY),
        ],
        out_shape=[
            jax.ShapeDtypeStruct((t, D_MODEL), f32),
            jax.ShapeDtypeStruct((SUBLANES, t), f32),
            jax.ShapeDtypeStruct((TOP_K, t), i32),
            jax.ShapeDtypeStruct((SUBLANES, nb_pad), i32),
            jax.ShapeDtypeStruct((SUBLANES, LANES), i32),
            jax.ShapeDtypeStruct((n_rows * TOKEN_ROWS, LANES), f32),
        ],
        scratch_shapes=[
            pltpu.VMEM((SEQ_TILE + CONV_HIST, D_CONV), f32),
            pltpu.VMEM((SUBLANES, SEQ_TILE + CONV_HIST - SUBLANES, D_CONV), f32),
            pltpu.VMEM((SEQ_TILE + LRU_HIST, D_LRU), f32),
            pltpu.VMEM((SEQ_TILE, D_CONV), f32),
            pltpu.VMEM((SEQ_TILE, D_LRU), f32),
            pltpu.VMEM((SEQ_TILE, D_LRU), f32),
            pltpu.VMEM((SEQ_TILE, D_LRU), f32),
            pltpu.VMEM((1, D_LRU), f32),
            pltpu.VMEM((SEQ_TILE, D_MODEL), jnp.bfloat16),
            pltpu.VMEM((N_EXPERTS, LANES), f32),
            pltpu.VMEM((N_EXPERTS, LANES), f32),
            pltpu.VMEM((N_EXPERTS, LANES), f32),
            pltpu.VMEM((SUBLANES, nb_pad), f32),
            pltpu.VMEM((2, SEQ_TILE * TOKEN_ROWS, LANES), f32),
            pltpu.VMEM((2, TOP_K, SEQ_TILE), i32),
            pltpu.SMEM((2, TOP_K, SEQ_TILE), i32),
            pltpu.VMEM((3, N_EXPERTS, LANES), i32),
            pltpu.SMEM((3, N_EXPERTS, LANES), i32),
            pltpu.SemaphoreType.DMA((2,)),
            pltpu.SemaphoreType.DMA(()),
            pltpu.SemaphoreType.DMA(()),
            pltpu.SemaphoreType.DMA(()),
            pltpu.SemaphoreType.DMA(()),
        ],
        compiler_params=pltpu.CompilerParams(
            dimension_semantics=("arbitrary", "arbitrary"),
            vmem_limit_bytes=VMEM_LIMIT),
        name="mixer_router",
    )(x2d, *weights)


def _expert_kernel(order_ref, be_ref, nu_ref, xs_ref, w1_ref, w3_ref, w2_ref, y_ref,
                   w1b_ref, w3b_ref, w2b_ref):
    del order_ref
    i = pl.program_id(0)
    used = i < nu_ref[0]

    @pl.when(used & ((i == 0) | (be_ref[i] != be_ref[jnp.maximum(i - 1, 0)])))
    def _():
        w1b_ref[...] = w1_ref[0].astype(jnp.bfloat16)
        w3b_ref[...] = w3_ref[0].astype(jnp.bfloat16)
        w2b_ref[...] = w2_ref[0].astype(jnp.bfloat16)

    @pl.when(used)
    def _():
        xb = _load_token_tiles(xs_ref, ROW_BLOCK).astype(jnp.bfloat16)
        h1 = jnp.dot(xb, w1b_ref[...], preferred_element_type=jnp.float32)
        h3 = jnp.dot(xb, w3b_ref[...], preferred_element_type=jnp.float32)
        act = (h1 * _sigmoid(h1) * h3).astype(jnp.bfloat16)
        _store_token_tiles(
            y_ref, jnp.dot(act, w2b_ref[...], preferred_element_type=jnp.float32))

    @pl.when(i >= nu_ref[0])
    def _():
        y_ref[...] = jnp.zeros_like(y_ref)


def _block_order(block_e, n_used):
    n = block_e.shape[0]
    blk = jnp.arange(n, dtype=jnp.int32)
    key = jnp.where(blk < n_used, block_e, N_EXPERTS)
    earlier = (key[None, :] < key[:, None]) | ((key[None, :] == key[:, None])
                                               & (blk[None, :] < blk[:, None]))
    pos = jnp.sum(earlier.astype(jnp.int32), axis=1)
    hit = pos[None, :] == blk[:, None]
    order = jnp.sum(jnp.where(hit, blk[None, :], 0), axis=1)
    sorted_e = jnp.sum(jnp.where(hit, key[None, :], 0), axis=1)
    return order, jnp.minimum(sorted_e, N_EXPERTS - 1)


def _experts(block_e, n_used, xs, w1, w3, w2):
    n_rows = xs.shape[0] // TOKEN_ROWS
    order, sorted_e = _block_order(block_e, n_used)
    rows = lambda i, od, be, nu: (od[i], 0)
    rows_in = lambda i, od, be, nu: (od[jnp.minimum(i, nu[0] - 1)], 0)
    wsel = lambda i, od, be, nu: (be[i], 0, 0)
    return pl.pallas_call(
        _expert_kernel,
        grid_spec=pltpu.PrefetchScalarGridSpec(
            num_scalar_prefetch=3,
            grid=(n_rows // ROW_BLOCK,),
            in_specs=[pl.BlockSpec((ROW_BLOCK * TOKEN_ROWS, LANES), rows_in),
                      pl.BlockSpec((1, D_MODEL, D_EXPERT), wsel),
                      pl.BlockSpec((1, D_MODEL, D_EXPERT), wsel),
                      pl.BlockSpec((1, D_EXPERT, D_MODEL), wsel)],
            out_specs=pl.BlockSpec((ROW_BLOCK * TOKEN_ROWS, LANES), rows),
            scratch_shapes=[pltpu.VMEM((D_MODEL, D_EXPERT), jnp.bfloat16),
                            pltpu.VMEM((D_MODEL, D_EXPERT), jnp.bfloat16),
                            pltpu.VMEM((D_EXPERT, D_MODEL), jnp.bfloat16)]),
        out_shape=jax.ShapeDtypeStruct((n_rows * TOKEN_ROWS, LANES), jnp.float32),
        compiler_params=pltpu.CompilerParams(
            dimension_semantics=("arbitrary",), vmem_limit_bytes=VMEM_LIMIT),
        name="expert_mlp",
    )(order, sorted_e, n_used, xs, w1, w3, w2)


def _tail_kernel(dcur_ref, dnext_ref, x1_ref, y_hbm, gate_ref, p_ref, wple_ref, gproj_ref,
                 gple_ref, wpg_ref, gfin_ref, o_ref, ybuf_ref, sem, *, final_norm):
    tt = TAIL_TILE
    i = pl.program_id(0)

    def tile_copy(d_ref, s, k, t, t0):
        return pltpu.make_async_copy(_token_tile(y_hbm, d_ref[k, t0 + t]),
                                     _token_tile(ybuf_ref.at[s], k * tt + t), sem.at[s])

    def wait_slot(s):
        pltpu.make_async_copy(y_hbm.at[pl.ds(0, TOP_K * tt * TOKEN_ROWS)], ybuf_ref.at[s],
                              sem.at[s]).wait()

    def issue_inline(d_ref, s, t0, part):
        for t in range(part * tt // TAIL_STAGES, (part + 1) * tt // TAIL_STAGES):
            for k in range(TOP_K):
                tile_copy(d_ref, s, k, t, t0).start(priority=k)

    def compute(s, r0, d_ref, s_next, t0_next):
        rows = slice(r0, r0 + tt)
        gpad = jnp.concatenate(
            [gate_ref[:, rows], jnp.zeros((LANES - SUBLANES, tt), jnp.float32)], axis=0)
        gt = gpad.T
        yb = ybuf_ref.at[s]
        issue_inline(d_ref, s_next, t0_next, 0)
        x2 = (x1_ref[rows, :] + gt[:, 0:1] * _load_token_tiles(yb, tt)
              + gt[:, 1:2] * _load_token_tiles(yb, tt, row0=tt * TOKEN_ROWS))
        issue_inline(d_ref, s_next, t0_next, 1)
        e = _rms(jnp.dot(p_ref[rows, :].astype(jnp.bfloat16), wple_ref[...],
                         preferred_element_type=jnp.float32), gproj_ref[...])
        hg = _rms(x2, gple_ref[...]).astype(jnp.bfloat16)
        issue_inline(d_ref, s_next, t0_next, 2)
        pg = _sigmoid(jnp.dot(hg, wpg_ref[...], preferred_element_type=jnp.float32))
        issue_inline(d_ref, s_next, t0_next, 3)
        x3 = x2 + pg * e
        o_ref[rows, :] = _rms(x3, gfin_ref[...]) if final_norm else x3

    @pl.when(i == 0)
    def _():
        def issue(t, carry):
            for k in range(TOP_K):
                tile_copy(dcur_ref, 0, k, t, 0).start()
            return carry
        lax.fori_loop(0, tt, issue, 0, unroll=DMA_UNROLL)

    wait_slot(0)
    compute(0, 0, dcur_ref, 1, tt)
    wait_slot(1)
    compute(1, tt, dnext_ref, 0, 0)

    @pl.when(i == pl.num_programs(0) - 1)
    def _():
        wait_slot(0)


def _tail(dest, x1, y, gate, p2d, wple, gproj, gple, wpg, gfin, final_norm):
    t = x1.shape[0]
    step = 2 * TAIL_TILE
    n_steps = t // step
    tok = lambda i: (i, 0)
    const = lambda i: (0, 0)

    def full(a):
        return pl.BlockSpec(a.shape, const)

    return pl.pallas_call(
        functools.partial(_tail_kernel, final_norm=final_norm),
        grid=(n_steps,),
        in_specs=[pl.BlockSpec((TOP_K, step), lambda i: (0, i), memory_space=pltpu.SMEM),
                  pl.BlockSpec((TOP_K, step), lambda i: (0, jnp.minimum(i + 1, n_steps - 1)),
                               memory_space=pltpu.SMEM),
                  pl.BlockSpec((step, D_MODEL), tok),
                  pl.BlockSpec(memory_space=pl.ANY),
                  pl.BlockSpec((SUBLANES, step), lambda i: (0, i)),
                  pl.BlockSpec((step, PLE_DIM), tok),
                  full(wple), full(gproj), full(gple), full(wpg), full(gfin)],
        out_specs=pl.BlockSpec((step, D_MODEL), tok),
        out_shape=jax.ShapeDtypeStruct((t, D_MODEL), jnp.float32),
        scratch_shapes=[pltpu.VMEM((2, TOP_K * TAIL_TILE * TOKEN_ROWS, LANES), jnp.float32),
                        pltpu.SemaphoreType.DMA((2,))],
        compiler_params=pltpu.CompilerParams(
            dimension_semantics=("arbitrary",), vmem_limit_bytes=VMEM_LIMIT),
        name="tail",
    )(dest, dest, x1, y, gate, p2d, wple, gproj, gple, wpg, gfin)


def _block_diag(w):
    h, hd, _ = w.shape
    eye = jnp.eye(h, dtype=w.dtype)
    return jnp.einsum('hij,hg->higj', w, eye).reshape(h * hd, h * hd)


def _router_weights(w_group, b_group, w_expert, b_expert):
    w = jnp.zeros((D_MODEL, ROUTER_COLS), jnp.float32)
    w = w.at[:, 0:N_GROUPS].set(w_group)
    w = w.at[:, EXPERT_COL0:EXPERT_COL0 + N_EXPERTS].set(w_expert)
    bias = jnp.zeros((1, ROUTER_COLS), jnp.float32)
    bias = bias.at[0, 0:N_GROUPS].set(b_group)
    bias = bias.at[0, EXPERT_COL0:EXPERT_COL0 + N_EXPERTS].set(b_expert)
    w_hi = w.astype(jnp.bfloat16)
    w_lo = (w - w_hi.astype(jnp.float32)).astype(jnp.bfloat16)
    return jnp.concatenate([w_hi, w_lo], axis=1), w_hi, bias


def _layer(x2d, p2d, batch, seq, final_norm, g_final, g_mix, w_in, conv_dw_w, conv_dw_b,
           conv_ln_g, conv_ln_b, lru_conv_w, lru_conv_b, lru_w_r, lru_b_r, lru_w_i, lru_b_i,
           lru_lambda, w_out, g_ffn, w_group, b_group, w_expert, b_expert, w1, w3, w2,
           g_ple, w_ple, g_ple_proj, w_ple_gate):
    bf16 = jnp.bfloat16
    t = batch * seq
    row = lambda v: v.reshape(1, -1)
    wg = jnp.concatenate([_block_diag(lru_w_r), _block_diag(lru_w_i)], axis=1).astype(bf16)
    bg = jnp.concatenate([lru_b_r, lru_b_i]).reshape(1, -1)
    wrc, wrh, br = _router_weights(w_group, b_group, w_expert, b_expert)
    ti = jnp.arange(SEQ_TILE)
    tri = (ti[:, None] < ti[None, :]).astype(bf16)
    ei = jnp.arange(N_EXPERTS)
    tril = (ei[None, :] < ei[:, None]).astype(bf16)
    n_blocks = (t * TOP_K) // ROW_BLOCK + N_EXPERTS
    n_rows = n_blocks * ROW_BLOCK

    x1, gate, dest, be, nu, xs = _mixer(
        x2d, batch, seq, n_rows, row(g_mix), w_in.astype(bf16), conv_dw_w, row(conv_dw_b),
        row(conv_ln_g), row(conv_ln_b), lru_conv_w, row(lru_conv_b), wg, bg,
        row(lru_lambda), w_out.astype(bf16), row(g_ffn), wrc, wrh, br, tri, tril)
    y = _experts(be[0, :n_blocks], nu[0, 0:1], xs, w1, w3, w2)
    return _tail(dest, x1, y, gate, p2d, w_ple.astype(bf16), row(g_ple_proj), row(g_ple),
                 w_ple_gate.astype(bf16), row(g_final), final_norm)


def kernel(x, p, g_mix, w_in, conv_dw_w, conv_dw_b, conv_ln_g, conv_ln_b, lru_conv_w,
           lru_conv_b, lru_w_r, lru_b_r, lru_w_i, lru_b_i, lru_lambda, w_out, g_ffn,
           w_group, b_group, w_expert, b_expert, w1, w3, w2, g_ple, w_ple, g_ple_proj,
           w_ple_gate, g_final):
    batch, seq, d = x.shape
    assert d == D_MODEL and seq % SEQ_TILE == 0 and SEQ_TILE == ROW_BLOCK
    assert (batch * seq) % max(2 * TAIL_TILE, ROW_BLOCK) == 0
    depth = w_in.shape[0]
    layers = (g_mix, w_in, conv_dw_w, conv_dw_b, conv_ln_g, conv_ln_b, lru_conv_w,
              lru_conv_b, lru_w_r, lru_b_r, lru_w_i, lru_b_i, lru_lambda, w_out, g_ffn,
              w_group, b_group, w_expert, b_expert, w1, w3, w2, g_ple, w_ple, g_ple_proj,
              w_ple_gate)
    x2d = x.reshape(batch * seq, d)
    for i in range(depth):
        x2d = _layer(x2d, p[i].reshape(batch * seq, PLE_DIM), batch, seq, i == depth - 1,
                     g_final, *(w[i] for w in layers))
    return x2d.reshape(batch, seq, d)
```

```python
import functools

import jax
import jax.numpy as jnp
from jax import lax
from jax.experimental import pallas as pl
from jax.experimental.pallas import tpu as pltpu

D_MODEL = 1024
D_CONV = 512
D_LRU = 512
LRU_HEADS = 8
LRU_HEAD_DIM = D_LRU // LRU_HEADS
CONV_WIDTH = 31
LRU_CONV_WIDTH = 4
LRU_C = 8.0
N_GROUPS = 4
EXPERTS_PER_GROUP = 8
N_EXPERTS = N_GROUPS * EXPERTS_PER_GROUP
TOP_K = 2
D_EXPERT = D_MODEL // 2
PLE_DIM = 256
EPS = 1e-6

SUBLANES = 8
LANES = 128

SEQ_TILE = 512
CONV_HIST = 32
LRU_HIST = SUBLANES
CONV_ROWS = 64
ROW_BLOCK = 512
DMA_UNROLL = 8
TAIL_TILE = 512
TAIL_STAGES = 4
ROUTER_COLS = LANES
EXPERT_COL0 = SUBLANES
VMEM_LIMIT = 56 * 1024 * 1024


def _sigmoid(v):
    return 1.0 / (1.0 + jnp.exp(-v))


def _rms(v, g):
    return v * lax.rsqrt(jnp.mean(v * v, axis=-1, keepdims=True) + EPS) * g


TOKEN_ROWS = D_MODEL // LANES


def _store_token_tiles(ref, v, row0=0):
    n = v.shape[0]
    for c in range(TOKEN_ROWS):
        ref[pl.ds(row0 + c, n, stride=TOKEN_ROWS), :] = v[:, c * LANES:(c + 1) * LANES]


def _load_token_tiles(ref, n, row0=0):
    return jnp.concatenate(
        [ref[pl.ds(row0 + c, n, stride=TOKEN_ROWS), :] for c in range(TOKEN_ROWS)], axis=1)


def _token_tile(ref, tile):
    start = tile * TOKEN_ROWS
    if not isinstance(tile, int):
        start = pl.multiple_of(start, TOKEN_ROWS)
    return ref.at[pl.ds(start, TOKEN_ROWS), :]


def _mixer_kernel(x_ref, gmix_ref, win_ref, cw_ref, cb_ref, lng_ref, lnb_ref,
                  lw_ref, lb_ref, wg_ref, bg_ref, lam_ref, wout_ref, gffn_ref,
                  wrc_ref, wrh_ref, br_ref, tri_ref, tril_ref,
                  x1_ref, gate_ref, dest_ref, be_ref, nu_ref, xs_hbm,
                  uh_ref, ush_ref, lxh_ref, cv_ref, a_ref, h_ref, gl_ref, hcar_ref,
                  mix_ref, tot_ref, cur_ref, nal_ref, bet_ref, hfbuf_ref, dvec_ref,
                  dsm_ref, zvec_ref, zsm_ref, dsem, ssem, zsem, bsem, isem):
    ts = SEQ_TILE
    b = pl.program_id(0)
    s = pl.program_id(1)
    step = b * pl.num_programs(1) + s
    last = pl.num_programs(0) * pl.num_programs(1) - 1
    slot = lax.rem(step, 2)
    prev = 1 - slot
    n_blocks = xs_hbm.shape[0] // (ROW_BLOCK * TOKEN_ROWS)
    inv_block = 1.0 / ROW_BLOCK

    def dest_copy(sl):
        return pltpu.make_async_copy(dvec_ref.at[sl], dsm_ref.at[sl], dsem.at[sl])

    def scatter_copy(sl, k, t):
        return pltpu.make_async_copy(_token_tile(hfbuf_ref.at[sl], t),
                                     _token_tile(xs_hbm, dsm_ref[sl, k, t]), ssem)

    def scatter_wait():
        rows = TOP_K * ts * TOKEN_ROWS
        pltpu.make_async_copy(xs_hbm.at[pl.ds(0, rows)], xs_hbm.at[pl.ds(0, rows)], ssem).wait()

    @pl.when(s == 0)
    def _():
        uh_ref[0:CONV_HIST, :] = jnp.zeros((CONV_HIST, D_CONV), jnp.float32)
        lxh_ref[0:LRU_HIST, :] = jnp.zeros((LRU_HIST, D_LRU), jnp.float32)
        hcar_ref[...] = jnp.zeros_like(hcar_ref)

    @pl.when(step == 0)
    def _():
        tot_ref[...] = jnp.zeros_like(tot_ref)
        cur_ref[...] = jnp.zeros_like(cur_ref)
        nal_ref[...] = jnp.zeros_like(nal_ref)
        bet_ref[...] = jnp.zeros_like(bet_ref)
        hfbuf_ref[1] = jnp.zeros((ts * TOKEN_ROWS, LANES), jnp.float32)
        dvec_ref[1] = (lax.broadcasted_iota(jnp.int32, (TOP_K, ts), 0) * ts
                       + lax.broadcasted_iota(jnp.int32, (TOP_K, ts), 1))
        dest_copy(1).start()

    x = x_ref[...]
    hb = _rms(x, gmix_ref[...]).astype(jnp.bfloat16)

    def zcols(j):
        return jnp.dot(hb, win_ref[:, j * D_CONV:(j + 1) * D_CONV],
                       preferred_element_type=jnp.float32)

    uh_ref[CONV_HIST:CONV_HIST + ts, :] = zcols(0) * _sigmoid(zcols(1))
    lxh_ref[LRU_HIST:LRU_HIST + ts, :] = zcols(2)
    gl_ref[...] = jax.nn.gelu(zcols(3), approximate=True)

    span = ts + CONV_HIST - SUBLANES
    u_ext = uh_ref[...]
    for r in range(1, SUBLANES):
        ush_ref[r] = pltpu.roll(u_ext, r, 0)[SUBLANES:SUBLANES + span, :]
    dest_copy(prev).wait()
    n_chunks = ts // CONV_ROWS
    for c in range(n_chunks):
        r0 = c * CONV_ROWS
        for lc in range(D_CONV // LANES):
            ls = slice(lc * LANES, (lc + 1) * LANES)
            acc = jnp.broadcast_to(cb_ref[:, ls], (CONV_ROWS, LANES))
            for j in range(CONV_WIDTH):
                k = CONV_WIDTH - 1 - j
                off = r0 + CONV_HIST - SUBLANES - SUBLANES * (j // SUBLANES)
                if j % SUBLANES == 0:
                    u_j = uh_ref[off + SUBLANES:off + SUBLANES + CONV_ROWS, ls]
                else:
                    u_j = ush_ref[j % SUBLANES, off:off + CONV_ROWS, ls]
                acc = acc + u_j * cw_ref[k:k + 1, ls]
            cv_ref[r0:r0 + CONV_ROWS, ls] = acc
        for t in range(c * ts // n_chunks, (c + 1) * ts // n_chunks):
            for k in range(TOP_K):
                scatter_copy(prev, k, t).start(priority=k)

    cv = cv_ref[...]
    mu = jnp.mean(cv, axis=-1, keepdims=True)
    dv = cv - mu
    var = jnp.mean(dv * dv, axis=-1, keepdims=True)
    yc = dv * lax.rsqrt(var + EPS) * lng_ref[...] + lnb_ref[...]
    mix_ref[:, 0:D_CONV] = (yc * _sigmoid(yc)).astype(jnp.bfloat16)

    xr = jnp.broadcast_to(lb_ref[...], (ts, D_LRU))
    for j in range(LRU_CONV_WIDTH):
        k = LRU_CONV_WIDTH - 1 - j
        xr = xr + lxh_ref[LRU_HIST - j:LRU_HIST - j + ts, :] * lw_ref[k:k + 1, :]
    gates = jnp.dot(xr.astype(jnp.bfloat16), wg_ref[...],
                    preferred_element_type=jnp.float32) + bg_ref[...]
    r_g = _sigmoid(gates[:, 0:D_LRU])
    i_g = _sigmoid(gates[:, D_LRU:2 * D_LRU])
    nlam = -lam_ref[...]
    softplus = jnp.maximum(nlam, 0.0) + jnp.log1p(jnp.exp(-jnp.abs(nlam)))
    log_a = -LRU_C * r_g * softplus
    a_ref[...] = jnp.exp(log_a)
    th = jnp.tanh(log_a)
    h_ref[...] = jnp.sqrt(-2.0 * th / (1.0 - th)) * (i_g * xr)

    row = lax.broadcasted_iota(jnp.int32, (SUBLANES, D_LRU), 0)

    def scan_group(g, hprev):
        r0 = pl.multiple_of(g * SUBLANES, SUBLANES)
        a = a_ref[pl.ds(r0, SUBLANES), :]
        hh = h_ref[pl.ds(r0, SUBLANES), :]
        for sh in (1, 2, 4):
            keep = row >= sh
            a_sh = jnp.where(keep, pltpu.roll(a, sh, 0), 1.0)
            h_sh = jnp.where(keep, pltpu.roll(hh, sh, 0), 0.0)
            hh = a * h_sh + hh
            a = a * a_sh
        hh = hh + a * hprev
        h_ref[pl.ds(r0, SUBLANES), :] = hh
        return hh[SUBLANES - 1:SUBLANES, :]

    hcar_ref[...] = lax.fori_loop(0, ts // SUBLANES, scan_group, hcar_ref[...])
    mix_ref[:, D_CONV:D_CONV + D_LRU] = (h_ref[...] * gl_ref[...]).astype(jnp.bfloat16)

    uh_ref[0:CONV_HIST, :] = uh_ref[ts:ts + CONV_HIST, :]
    lxh_ref[0:LRU_HIST, :] = lxh_ref[ts:ts + LRU_HIST, :]

    x1 = x + jnp.dot(mix_ref[...], wout_ref[...], preferred_element_type=jnp.float32)
    x1_ref[...] = x1
    hf = _rms(x1, gffn_ref[...])
    _store_token_tiles(hfbuf_ref.at[slot], hf)

    hf_hi = hf.astype(jnp.bfloat16)
    hf_lo = (hf - hf_hi.astype(jnp.float32)).astype(jnp.bfloat16)
    hi_both = jnp.dot(hf_hi, wrc_ref[...], preferred_element_type=jnp.float32)
    logits = (hi_both[:, 0:ROUTER_COLS] + hi_both[:, ROUTER_COLS:2 * ROUTER_COLS]
              + jnp.dot(hf_lo, wrh_ref[...], preferred_element_type=jnp.float32)
              + br_ref[...])
    lt = logits.T

    gl = [lt[g:g + 1, :] for g in range(N_GROUPS)]
    gmax = functools.reduce(jnp.maximum, gl)
    gidx = jnp.full((1, ts), N_GROUPS - 1, jnp.int32)
    for g in range(N_GROUPS - 2, -1, -1):
        gidx = jnp.where(gl[g] == gmax, g, gidx)
    gsum = functools.reduce(lambda p, q: p + q, [jnp.exp(v - gmax) for v in gl])
    gprob = 1.0 / gsum

    def eblock(g):
        lo = EXPERT_COL0 + g * EXPERTS_PER_GROUP
        return lt[lo:lo + EXPERTS_PER_GROUP, :]

    esel = eblock(N_GROUPS - 1)
    for g in range(N_GROUPS - 2, -1, -1):
        esel = jnp.where(gidx == g, eblock(g), esel)
    ridx = lax.broadcasted_iota(jnp.int32, (EXPERTS_PER_GROUP, ts), 0)
    m1 = jnp.max(esel, axis=0, keepdims=True)
    i1 = jnp.min(jnp.where(esel == m1, ridx, EXPERTS_PER_GROUP), axis=0, keepdims=True)
    rest = jnp.where(ridx == i1, -jnp.inf, esel)
    m2 = jnp.max(rest, axis=0, keepdims=True)
    i2 = jnp.min(jnp.where(rest == m2, ridx, EXPERTS_PER_GROUP), axis=0, keepdims=True)
    p2 = jnp.exp(m2 - m1)
    den = 1.0 + p2
    gate_ref[...] = jnp.concatenate(
        [gprob / den, gprob * p2 / den,
         jnp.zeros((SUBLANES - TOP_K, ts), jnp.float32)], axis=0)
    eid0 = gidx * EXPERTS_PER_GROUP + i1
    eid1 = gidx * EXPERTS_PER_GROUP + i2

    eio = lax.broadcasted_iota(jnp.int32, (N_EXPERTS, ts), 0)
    oh0 = jnp.where(eio == eid0, 1.0, 0.0)
    oh1 = jnp.where(eio == eid1, 1.0, 0.0)
    tri = tri_ref[...]
    pre0 = jnp.dot(oh0.astype(jnp.bfloat16), tri, preferred_element_type=jnp.float32)
    pre1 = jnp.dot(oh1.astype(jnp.bfloat16), tri, preferred_element_type=jnp.float32)
    c0 = jnp.sum(oh0, axis=1, keepdims=True)
    c1 = jnp.sum(oh1, axis=1, keepdims=True)
    before = tot_ref[:, 0:1]
    rank0 = jnp.sum(oh0 * (before + pre0), axis=0, keepdims=True)
    rank1 = jnp.sum(oh1 * (before + c0 + pre1), axis=0, keepdims=True)

    after = before + c0 + c1
    nbb = jnp.floor((before + (ROW_BLOCK - 1)) * inv_block)
    nba = jnp.floor((after + (ROW_BLOCK - 1)) * inv_block)
    new = nba - nbb
    new_wide = jnp.broadcast_to(new, (N_EXPERTS, LANES))
    newbase = nal_ref[:, 0:1] + jnp.dot(tril_ref[...], new_wide.astype(jnp.bfloat16),
                                         preferred_element_type=jnp.float32)[:, 0:1]
    curb = cur_ref[:, 0:1]
    voff = newbase - nbb

    def place(oh, p):
        def look(col):
            return jnp.sum(oh * col, axis=0, keepdims=True)
        q = jnp.floor(p * inv_block)
        blk = jnp.where(q < look(nbb), look(curb), look(voff) + q)
        return blk * ROW_BLOCK + (p - q * ROW_BLOCK)

    dest = jnp.concatenate([place(oh0, rank0), place(oh1, rank1)], axis=0).astype(jnp.int32)
    dest_ref[...] = dest
    dvec_ref[slot] = dest
    dest_copy(slot).start()

    tot_ref[...] = jnp.broadcast_to(after, (N_EXPERTS, LANES))
    cur_ref[...] = jnp.broadcast_to(jnp.where(new > 0, newbase + new - 1.0, curb),
                                    (N_EXPERTS, LANES))
    nal_new = nal_ref[...] + jnp.sum(new_wide, axis=0, keepdims=True)
    nal_ref[...] = nal_new
    nb_pad = bet_ref.shape[1]
    jcol = lax.broadcasted_iota(jnp.int32, (N_EXPERTS, nb_pad), 1).astype(jnp.float32)
    erow = lax.broadcasted_iota(jnp.int32, (N_EXPERTS, nb_pad), 0).astype(jnp.float32)
    mine = jnp.where(jcol >= newbase, 1.0, 0.0) * jnp.where(jcol < newbase + new, 1.0, 0.0)
    owner = jnp.sum(mine * erow, axis=0, keepdims=True)
    taken = jnp.sum(mine, axis=0, keepdims=True)
    bet = jnp.where(taken > 0.0, owner, bet_ref[0:1, :])
    bet_ref[...] = jnp.broadcast_to(bet, bet_ref.shape)
    be_ref[...] = jnp.broadcast_to(bet, bet_ref.shape).astype(jnp.int32)
    nu_ref[...] = nal_new[0:SUBLANES, :].astype(jnp.int32)

    scatter_wait()

    @pl.when(step == last)
    def _():
        dest_copy(slot).wait()

        def issue(t, carry):
            for k in range(TOP_K):
                scatter_copy(slot, k, t).start(priority=k)
            return carry
        lax.fori_loop(0, ts, issue, 0, unroll=DMA_UNROLL)
        scatter_wait()

        fill = after - ROW_BLOCK * jnp.floor(after * inv_block)
        zvec_ref[0] = cur_ref[...].astype(jnp.int32)
        zvec_ref[1] = jnp.broadcast_to(fill, (N_EXPERTS, LANES)).astype(jnp.int32)
        zvec_ref[2] = nal_new.astype(jnp.int32)
        info = pltpu.make_async_copy(zvec_ref, zsm_ref, isem)
        info.start()
        zsrc = hfbuf_ref.at[prev]
        zsrc[...] = jnp.zeros((ts * TOKEN_ROWS, LANES), jnp.float32)
        info.wait()

        def zero_tile(r):
            return pltpu.make_async_copy(_token_tile(zsrc, 0), _token_tile(xs_hbm, r), zsem)

        def per_expert(e, total):
            filled = zsm_ref[1, e, 0]
            lo = jnp.where(filled > 0, filled, ROW_BLOCK)
            row0 = zsm_ref[0, e, 0] * ROW_BLOCK

            def issue_zero(r, carry):
                zero_tile(row0 + r).start()
                return carry
            lax.fori_loop(lo, ROW_BLOCK, issue_zero, 0)
            return total + (ROW_BLOCK - lo)
        total = lax.fori_loop(0, N_EXPERTS, per_expert, 0)

        def drain(r, carry):
            zero_tile(0).wait()
            return carry
        lax.fori_loop(0, total, drain, 0)

        block_rows = ROW_BLOCK * TOKEN_ROWS
        assert block_rows == ts * TOKEN_ROWS

        def zero_block(blk):
            start = pl.multiple_of(blk * block_rows, block_rows)
            return pltpu.make_async_copy(zsrc, xs_hbm.at[pl.ds(start, block_rows), :], bsem)

        def issue_block(blk, carry):
            zero_block(blk).start()
            return carry
        lax.fori_loop(zsm_ref[2, 0, 0], n_blocks, issue_block, 0)

        def drain_block(blk, carry):
            zero_block(blk).wait()
            return carry
        lax.fori_loop(zsm_ref[2, 0, 0], n_blocks, drain_block, 0)


def _mixer(x2d, batch, seq, n_rows, gmix, win, cw, cb, lng, lnb, lw, lb, wg, bg, lam, wout,
           gffn, wrc, wrh, br, tri, tril):
    t = batch * seq
    ns = seq // SEQ_TILE
    nb_pad = -(-(n_rows // ROW_BLOCK) // LANES) * LANES
    tok = lambda b, s: (b * ns + s, 0)
    tokl = lambda b, s: (0, b * ns + s)
    const = lambda b, s: (0, 0)

    def full(a):
        return pl.BlockSpec(a.shape, const)

    weights = (gmix, win, cw, cb, lng, lnb, lw, lb, wg, bg, lam, wout, gffn, wrc, wrh, br,
               tri, tril)
    f32, i32 = jnp.float32, jnp.int32
    return pl.pallas_call(
        _mixer_kernel,
        grid=(batch, ns),
        in_specs=[pl.BlockSpec((SEQ_TILE, D_MODEL), tok)] + [full(a) for a in weights],
        out_specs=[
            pl.BlockSpec((SEQ_TILE, D_MODEL), tok),
            pl.BlockSpec((SUBLANES, SEQ_TILE), tokl),
            pl.BlockSpec((TOP_K, SEQ_TILE), tokl),
            pl.BlockSpec((SUBLANES, nb_pad), const),
            pl.BlockSpec((SUBLANES, LANES), const),
            pl.BlockSpec(memory_space=pl.ANY),
        ],
        out_shape=[
            jax.ShapeDtypeStruct((t, D_MODEL), f32),
            jax.ShapeDtypeStruct((SUBLANES, t), f32),
            jax.ShapeDtypeStruct((TOP_K, t), i32),
            jax.ShapeDtypeStruct((SUBLANES, nb_pad), i32),
            jax.ShapeDtypeStruct((SUBLANES, LANES), i32),
            jax.ShapeDtypeStruct((n_rows * TOKEN_ROWS, LANES), f32),
        ],
        scratch_shapes=[
            pltpu.VMEM((SEQ_TILE + CONV_HIST, D_CONV), f32),
            pltpu.VMEM((SUBLANES, SEQ_TILE + CONV_HIST - SUBLANES, D_CONV), f32),
            pltpu.VMEM((SEQ_TILE + LRU_HIST, D_LRU), f32),
            pltpu.VMEM((SEQ_TILE, D_CONV), f32),
            pltpu.VMEM((SEQ_TILE, D_LRU), f32),
            pltpu.VMEM((SEQ_TILE, D_LRU), f32),
            pltpu.VMEM((SEQ_TILE, D_LRU), f32),
            pltpu.VMEM((1, D_LRU), f32),
            pltpu.VMEM((SEQ_TILE, D_MODEL), jnp.bfloat16),
            pltpu.VMEM((N_EXPERTS, LANES), f32),
            pltpu.VMEM((N_EXPERTS, LANES), f32),
            pltpu.VMEM((N_EXPERTS, LANES), f32),
            pltpu.VMEM((SUBLANES, nb_pad), f32),
            pltpu.VMEM((2, SEQ_TILE * TOKEN_ROWS, LANES), f32),
            pltpu.VMEM((2, TOP_K, SEQ_TILE), i32),
            pltpu.SMEM((2, TOP_K, SEQ_TILE), i32),
            pltpu.VMEM((3, N_EXPERTS, LANES), i32),
            pltpu.SMEM((3, N_EXPERTS, LANES), i32),
            pltpu.SemaphoreType.DMA((2,)),
            pltpu.SemaphoreType.DMA(()),
            pltpu.SemaphoreType.DMA(()),
            pltpu.SemaphoreType.DMA(()),
            pltpu.SemaphoreType.DMA(()),
        ],
        compiler_params=pltpu.CompilerParams(
            dimension_semantics=("arbitrary", "arbitrary"),
            vmem_limit_bytes=VMEM_LIMIT),
        name="mixer_router",
    )(x2d, *weights)


def _expert_kernel(order_ref, be_ref, nu_ref, xs_ref, w1_ref, w3_ref, w2_ref, y_ref,
                   w1b_ref, w3b_ref, w2b_ref):
    del order_ref
    i = pl.program_id(0)
    used = i < nu_ref[0]

    @pl.when(used & ((i == 0) | (be_ref[i] != be_ref[jnp.maximum(i - 1, 0)])))
    def _():
        w1b_ref[...] = w1_ref[0].astype(jnp.bfloat16)
        w3b_ref[...] = w3_ref[0].astype(jnp.bfloat16)
        w2b_ref[...] = w2_ref[0].astype(jnp.bfloat16)

    @pl.when(used)
    def _():
        xb = _load_token_tiles(xs_ref, ROW_BLOCK).astype(jnp.bfloat16)
        h1 = jnp.dot(xb, w1b_ref[...], preferred_element_type=jnp.float32)
        h3 = jnp.dot(xb, w3b_ref[...], preferred_element_type=jnp.float32)
        act = (h1 * _sigmoid(h1) * h3).astype(jnp.bfloat16)
        _store_token_tiles(
            y_ref, jnp.dot(act, w2b_ref[...], preferred_element_type=jnp.float32))

    @pl.when(i >= nu_ref[0])
    def _():
        y_ref[...] = jnp.zeros_like(y_ref)


def _block_order(block_e, n_used):
    n = block_e.shape[0]
    blk = jnp.arange(n, dtype=jnp.int32)
    key = jnp.where(blk < n_used, block_e, N_EXPERTS)
    earlier = (key[None, :] < key[:, None]) | ((key[None, :] == key[:, None])
                                               & (blk[None, :] < blk[:, None]))
    pos = jnp.sum(earlier.astype(jnp.int32), axis=1)
    hit = pos[None, :] == blk[:, None]
    order = jnp.sum(jnp.where(hit, blk[None, :], 0), axis=1)
    sorted_e = jnp.sum(jnp.where(hit, key[None, :], 0), axis=1)
    return order, jnp.minimum(sorted_e, N_EXPERTS - 1)


def _experts(block_e, n_used, xs, w1, w3, w2):
    n_rows = xs.shape[0] // TOKEN_ROWS
    order, sorted_e = _block_order(block_e, n_used)
    rows = lambda i, od, be, nu: (od[i], 0)
    rows_in = lambda i, od, be, nu: (od[jnp.minimum(i, nu[0] - 1)], 0)
    wsel = lambda i, od, be, nu: (be[i], 0, 0)
    return pl.pallas_call(
        _expert_kernel,
        grid_spec=pltpu.PrefetchScalarGridSpec(
            num_scalar_prefetch=3,
            grid=(n_rows // ROW_BLOCK,),
            in_specs=[pl.BlockSpec((ROW_BLOCK * TOKEN_ROWS, LANES), rows_in),
                      pl.BlockSpec((1, D_MODEL, D_EXPERT), wsel),
                      pl.BlockSpec((1, D_MODEL, D_EXPERT), wsel),
                      pl.BlockSpec((1, D_EXPERT, D_MODEL), wsel)],
            out_specs=pl.BlockSpec((ROW_BLOCK * TOKEN_ROWS, LANES), rows),
            scratch_shapes=[pltpu.VMEM((D_MODEL, D_EXPERT), jnp.bfloat16),
                            pltpu.VMEM((D_MODEL, D_EXPERT), jnp.bfloat16),
                            pltpu.VMEM((D_EXPERT, D_MODEL), jnp.bfloat16)]),
        out_shape=jax.ShapeDtypeStruct((n_rows * TOKEN_ROWS, LANES), jnp.float32),
        compiler_params=pltpu.CompilerParams(
            dimension_semantics=("arbitrary",), vmem_limit_bytes=VMEM_LIMIT),
        name="expert_mlp",
    )(order, sorted_e, n_used, xs, w1, w3, w2)


def _tail_kernel(dcur_ref, dnext_ref, x1_ref, y_hbm, gate_ref, p_ref, wple_ref, gproj_ref,
                 gple_ref, wpg_ref, gfin_ref, o_ref, ybuf_ref, sem, *, final_norm):
    tt = TAIL_TILE
    i = pl.program_id(0)

    def tile_copy(d_ref, s, k, t, t0):
        return pltpu.make_async_copy(_token_tile(y_hbm, d_ref[k, t0 + t]),
                                     _token_tile(ybuf_ref.at[s], k * tt + t), sem.at[s])

    def wait_slot(s):
        pltpu.make_async_copy(y_hbm.at[pl.ds(0, TOP_K * tt * TOKEN_ROWS)], ybuf_ref.at[s],
                              sem.at[s]).wait()

    def issue_inline(d_ref, s, t0, part):
        for t in range(part * tt // TAIL_STAGES, (part + 1) * tt // TAIL_STAGES):
            for k in range(TOP_K):
                tile_copy(d_ref, s, k, t, t0).start(priority=k)

    def compute(s, r0, d_ref, s_next, t0_next):
        rows = slice(r0, r0 + tt)
        gpad = jnp.concatenate(
            [gate_ref[:, rows], jnp.zeros((LANES - SUBLANES, tt), jnp.float32)], axis=0)
        gt = gpad.T
        yb = ybuf_ref.at[s]
        issue_inline(d_ref, s_next, t0_next, 0)
        x2 = (x1_ref[rows, :] + gt[:, 0:1] * _load_token_tiles(yb, tt)
              + gt[:, 1:2] * _load_token_tiles(yb, tt, row0=tt * TOKEN_ROWS))
        issue_inline(d_ref, s_next, t0_next, 1)
        e = _rms(jnp.dot(p_ref[rows, :].astype(jnp.bfloat16), wple_ref[...],
                         preferred_element_type=jnp.float32), gproj_ref[...])
        hg = _rms(x2, gple_ref[...]).astype(jnp.bfloat16)
        issue_inline(d_ref, s_next, t0_next, 2)
        pg = _sigmoid(jnp.dot(hg, wpg_ref[...], preferred_element_type=jnp.float32))
        issue_inline(d_ref, s_next, t0_next, 3)
        x3 = x2 + pg * e
        o_ref[rows, :] = _rms(x3, gfin_ref[...]) if final_norm else x3

    @pl.when(i == 0)
    def _():
        def issue(t, carry):
            for k in range(TOP_K):
                tile_copy(dcur_ref, 0, k, t, 0).start()
            return carry
        lax.fori_loop(0, tt, issue, 0, unroll=DMA_UNROLL)

    wait_slot(0)
    compute(0, 0, dcur_ref, 1, tt)
    wait_slot(1)
    compute(1, tt, dnext_ref, 0, 0)

    @pl.when(i == pl.num_programs(0) - 1)
    def _():
        wait_slot(0)


def _tail(dest, x1, y, gate, p2d, wple, gproj, gple, wpg, gfin, final_norm):
    t = x1.shape[0]
    step = 2 * TAIL_TILE
    n_steps = t // step
    tok = lambda i: (i, 0)
    const = lambda i: (0, 0)

    def full(a):
        return pl.BlockSpec(a.shape, const)

    return pl.pallas_call(
        functools.partial(_tail_kernel, final_norm=final_norm),
        grid=(n_steps,),
        in_specs=[pl.BlockSpec((TOP_K, step), lambda i: (0, i), memory_space=pltpu.SMEM),
                  pl.BlockSpec((TOP_K, step), lambda i: (0, jnp.minimum(i + 1, n_steps - 1)),
                               memory_space=pltpu.SMEM),
                  pl.BlockSpec((step, D_MODEL), tok),
                  pl.BlockSpec(memory_space=pl.ANY),
                  pl.BlockSpec((SUBLANES, step), lambda i: (0, i)),
                  pl.BlockSpec((step, PLE_DIM), tok),
                  full(wple), full(gproj), full(gple), full(wpg), full(gfin)],
        out_specs=pl.BlockSpec((step, D_MODEL), tok),
        out_shape=jax.ShapeDtypeStruct((t, D_MODEL), jnp.float32),
        scratch_shapes=[pltpu.VMEM((2, TOP_K * TAIL_TILE * TOKEN_ROWS, LANES), jnp.float32),
                        pltpu.SemaphoreType.DMA((2,))],
        compiler_params=pltpu.CompilerParams(
            dimension_semantics=("arbitrary",), vmem_limit_bytes=VMEM_LIMIT),
        name="tail",
    )(dest, dest, x1, y, gate, p2d, wple, gproj, gple, wpg, gfin)


def _block_diag(w):
    h, hd, _ = w.shape
    eye = jnp.eye(h, dtype=w.dtype)
    return jnp.einsum('hij,hg->higj', w, eye).reshape(h * hd, h * hd)


def _router_weights(w_group, b_group, w_expert, b_expert):
    w = jnp.zeros((D_MODEL, ROUTER_COLS), jnp.float32)
    w = w.at[:, 0:N_GROUPS].set(w_group)
    w = w.at[:, EXPERT_COL0:EXPERT_COL0 + N_EXPERTS].set(w_expert)
    bias = jnp.zeros((1, ROUTER_COLS), jnp.float32)
    bias = bias.at[0, 0:N_GROUPS].set(b_group)
    bias = bias.at[0, EXPERT_COL0:EXPERT_COL0 + N_EXPERTS].set(b_expert)
    w_hi = w.astype(jnp.bfloat16)
    w_lo = (w - w_hi.astype(jnp.float32)).astype(jnp.bfloat16)
    return jnp.concatenate([w_hi, w_lo], axis=1), w_hi, bias


def _layer(x2d, p2d, batch, seq, final_norm, g_final, g_mix, w_in, conv_dw_w, conv_dw_b,
           conv_ln_g, conv_ln_b, lru_conv_w, lru_conv_b, lru_w_r, lru_b_r, lru_w_i, lru_b_i,
           lru_lambda, w_out, g_ffn, w_group, b_group, w_expert, b_expert, w1, w3, w2,
           g_ple, w_ple, g_ple_proj, w_ple_gate):
    bf16 = jnp.bfloat16
    t = batch * seq
    row = lambda v: v.reshape(1, -1)
    wg = jnp.concatenate([_block_diag(lru_w_r), _block_diag(lru_w_i)], axis=1).astype(bf16)
    bg = jnp.concatenate([lru_b_r, lru_b_i]).reshape(1, -1)
    wrc, wrh, br = _router_weights(w_group, b_group, w_expert, b_expert)
    ti = jnp.arange(SEQ_TILE)
    tri = (ti[:, None] < ti[None, :]).astype(bf16)
    ei = jnp.arange(N_EXPERTS)
    tril = (ei[None, :] < ei[:, None]).astype(bf16)
    n_blocks = (t * TOP_K) // ROW_BLOCK + N_EXPERTS
    n_rows = n_blocks * ROW_BLOCK

    x1, gate, dest, be, nu, xs = _mixer(
        x2d, batch, seq, n_rows, row(g_mix), w_in.astype(bf16), conv_dw_w, row(conv_dw_b),
        row(conv_ln_g), row(conv_ln_b), lru_conv_w, row(lru_conv_b), wg, bg,
        row(lru_lambda), w_out.astype(bf16), row(g_ffn), wrc, wrh, br, tri, tril)
    y = _experts(be[0, :n_blocks], nu[0, 0:1], xs, w1, w3, w2)
    return _tail(dest, x1, y, gate, p2d, w_ple.astype(bf16), row(g_ple_proj), row(g_ple),
                 w_ple_gate.astype(bf16), row(g_final), final_norm)


def kernel(x, p, g_mix, w_in, conv_dw_w, conv_dw_b, conv_ln_g, conv_ln_b, lru_conv_w,
           lru_conv_b, lru_w_r, lru_b_r, lru_w_i, lru_b_i, lru_lambda, w_out, g_ffn,
           w_group, b_group, w_expert, b_expert, w1, w3, w2, g_ple, w_ple, g_ple_proj,
           w_ple_gate, g_final):
    batch, seq, d = x.shape
    assert d == D_MODEL and seq % SEQ_TILE == 0 and SEQ_TILE == ROW_BLOCK
    assert (batch * seq) % max(2 * TAIL_TILE, ROW_BLOCK) == 0
    depth = w_in.shape[0]
    layers = (g_mix, w_in, conv_dw_w, conv_dw_b, conv_ln_g, conv_ln_b, lru_conv_w,
              lru_conv_b, lru_w_r, lru_b_r, lru_w_i, lru_b_i, lru_lambda, w_out, g_ffn,
              w_group, b_group, w_expert, b_expert, w1, w3, w2, g_ple, w_ple, g_ple_proj,
              w_ple_gate)
    x2d = x.reshape(batch * seq, d)
    for i in range(depth):
        x2d = _layer(x2d, p[i].reshape(batch * seq, PLE_DIM), batch, seq, i == depth - 1,
                     g_final, *(w[i] for w in layers))
    return x2d.reshape(batch, seq, d)
```

```python
import functools

import jax
import jax.numpy as jnp
from jax import lax
from jax.experimental import pallas as pl
from jax.experimental.pallas import tpu as pltpu

D_MODEL = 1024
D_CONV = 512
D_LRU = 512
LRU_HEADS = 8
LRU_HEAD_DIM = D_LRU // LRU_HEADS
CONV_WIDTH = 31
LRU_CONV_WIDTH = 4
LRU_C = 8.0
N_GROUPS = 4
EXPERTS_PER_GROUP = 8
N_EXPERTS = N_GROUPS * EXPERTS_PER_GROUP
TOP_K = 2
D_EXPERT = D_MODEL // 2
PLE_DIM = 256
EPS = 1e-6

SUBLANES = 8
LANES = 128

SEQ_TILE = 512
CONV_HIST = 32
LRU_HIST = SUBLANES
CONV_ROWS = 32
ROW_BLOCK = 512
DMA_UNROLL = 8
TAIL_TILE = 512
TAIL_STAGES = 4
ROUTER_COLS = LANES
EXPERT_COL0 = SUBLANES
VMEM_LIMIT = 56 * 1024 * 1024


def _sigmoid(v):
    return 1.0 / (1.0 + jnp.exp(-v))


def _rms(v, g):
    return v * lax.rsqrt(jnp.mean(v * v, axis=-1, keepdims=True) + EPS) * g


TOKEN_ROWS = D_MODEL // LANES


def _store_token_tiles(ref, v, row0=0):
    n = v.shape[0]
    for c in range(TOKEN_ROWS):
        ref[pl.ds(row0 + c, n, stride=TOKEN_ROWS), :] = v[:, c * LANES:(c + 1) * LANES]


def _load_token_tiles(ref, n, row0=0):
    return jnp.concatenate(
        [ref[pl.ds(row0 + c, n, stride=TOKEN_ROWS), :] for c in range(TOKEN_ROWS)], axis=1)


def _token_tile(ref, tile):
    start = tile * TOKEN_ROWS
    if not isinstance(tile, int):
        start = pl.multiple_of(start, TOKEN_ROWS)
    return ref.at[pl.ds(start, TOKEN_ROWS), :]


def _mixer_kernel(x_ref, gmix_ref, win_ref, cw_ref, cb_ref, lng_ref, lnb_ref,
                  lw_ref, lb_ref, wg_ref, bg_ref, lam_ref, wout_ref, gffn_ref,
                  wrc_ref, wrh_ref, br_ref, tri_ref, tril_ref,
                  x1_ref, gate_ref, dest_ref, be_ref, nu_ref, xs_hbm,
                  uh_ref, ush_ref, lxh_ref, cv_ref, a_ref, h_ref, gl_ref, hcar_ref,
                  mix_ref, tot_ref, cur_ref, nal_ref, bet_ref, hfbuf_ref, dvec_ref,
                  dsm_ref, zvec_ref, zsm_ref, dsem, ssem, zsem, bsem, isem):
    ts = SEQ_TILE
    b = pl.program_id(0)
    s = pl.program_id(1)
    step = b * pl.num_programs(1) + s
    last = pl.num_programs(0) * pl.num_programs(1) - 1
    slot = lax.rem(step, 2)
    prev = 1 - slot
    n_blocks = xs_hbm.shape[0] // (ROW_BLOCK * TOKEN_ROWS)
    inv_block = 1.0 / ROW_BLOCK

    def dest_copy(sl):
        return pltpu.make_async_copy(dvec_ref.at[sl], dsm_ref.at[sl], dsem.at[sl])

    def scatter_copy(sl, k, t):
        return pltpu.make_async_copy(_token_tile(hfbuf_ref.at[sl], t),
                                     _token_tile(xs_hbm, dsm_ref[sl, k, t]), ssem)

    def scatter_wait():
        rows = TOP_K * ts * TOKEN_ROWS
        pltpu.make_async_copy(xs_hbm.at[pl.ds(0, rows)], xs_hbm.at[pl.ds(0, rows)], ssem).wait()

    @pl.when(s == 0)
    def _():
        uh_ref[0:CONV_HIST, :] = jnp.zeros((CONV_HIST, D_CONV), jnp.float32)
        lxh_ref[0:LRU_HIST, :] = jnp.zeros((LRU_HIST, D_LRU), jnp.float32)
        hcar_ref[...] = jnp.zeros_like(hcar_ref)

    @pl.when(step == 0)
    def _():
        tot_ref[...] = jnp.zeros_like(tot_ref)
        cur_ref[...] = jnp.zeros_like(cur_ref)
        nal_ref[...] = jnp.zeros_like(nal_ref)
        bet_ref[...] = jnp.zeros_like(bet_ref)
        hfbuf_ref[1] = jnp.zeros((ts * TOKEN_ROWS, LANES), jnp.float32)
        dvec_ref[1] = (lax.broadcasted_iota(jnp.int32, (TOP_K, ts), 0) * ts
                       + lax.broadcasted_iota(jnp.int32, (TOP_K, ts), 1))
        dest_copy(1).start()

    x = x_ref[...]
    hb = _rms(x, gmix_ref[...]).astype(jnp.bfloat16)

    def zcols(j):
        return jnp.dot(hb, win_ref[:, j * D_CONV:(j + 1) * D_CONV],
                       preferred_element_type=jnp.float32)

    uh_ref[CONV_HIST:CONV_HIST + ts, :] = zcols(0) * _sigmoid(zcols(1))
    lxh_ref[LRU_HIST:LRU_HIST + ts, :] = zcols(2)
    gl_ref[...] = jax.nn.gelu(zcols(3), approximate=True)

    span = ts + CONV_HIST - SUBLANES
    u_ext = uh_ref[...]
    for r in range(1, SUBLANES):
        ush_ref[r] = pltpu.roll(u_ext, r, 0)[SUBLANES:SUBLANES + span, :]
    dest_copy(prev).wait()
    n_chunks = ts // CONV_ROWS
    for c in range(n_chunks):
        r0 = c * CONV_ROWS
        for lc in range(D_CONV // LANES):
            ls = slice(lc * LANES, (lc + 1) * LANES)
            acc = jnp.broadcast_to(cb_ref[:, ls], (CONV_ROWS, LANES))
            for j in range(CONV_WIDTH):
                k = CONV_WIDTH - 1 - j
                off = r0 + CONV_HIST - SUBLANES - SUBLANES * (j // SUBLANES)
                if j % SUBLANES == 0:
                    u_j = uh_ref[off + SUBLANES:off + SUBLANES + CONV_ROWS, ls]
                else:
                    u_j = ush_ref[j % SUBLANES, off:off + CONV_ROWS, ls]
                acc = acc + u_j * cw_ref[k:k + 1, ls]
            cv_ref[r0:r0 + CONV_ROWS, ls] = acc
        for t in range(c * ts // n_chunks, (c + 1) * ts // n_chunks):
            for k in range(TOP_K):
                scatter_copy(prev, k, t).start(priority=k)

    cv = cv_ref[...]
    mu = jnp.mean(cv, axis=-1, keepdims=True)
    dv = cv - mu
    var = jnp.mean(dv * dv, axis=-1, keepdims=True)
    yc = dv * lax.rsqrt(var + EPS) * lng_ref[...] + lnb_ref[...]
    mix_ref[:, 0:D_CONV] = (yc * _sigmoid(yc)).astype(jnp.bfloat16)

    xr = jnp.broadcast_to(lb_ref[...], (ts, D_LRU))
    for j in range(LRU_CONV_WIDTH):
        k = LRU_CONV_WIDTH - 1 - j
        xr = xr + lxh_ref[LRU_HIST - j:LRU_HIST - j + ts, :] * lw_ref[k:k + 1, :]
    gates = jnp.dot(xr.astype(jnp.bfloat16), wg_ref[...],
                    preferred_element_type=jnp.float32) + bg_ref[...]
    r_g = _sigmoid(gates[:, 0:D_LRU])
    i_g = _sigmoid(gates[:, D_LRU:2 * D_LRU])
    nlam = -lam_ref[...]
    softplus = jnp.maximum(nlam, 0.0) + jnp.log1p(jnp.exp(-jnp.abs(nlam)))
    log_a = -LRU_C * r_g * softplus
    a_ref[...] = jnp.exp(log_a)
    th = jnp.tanh(log_a)
    h_ref[...] = jnp.sqrt(-2.0 * th / (1.0 - th)) * (i_g * xr)

    row = lax.broadcasted_iota(jnp.int32, (SUBLANES, D_LRU), 0)

    def scan_group(g, hprev):
        r0 = pl.multiple_of(g * SUBLANES, SUBLANES)
        a = a_ref[pl.ds(r0, SUBLANES), :]
        hh = h_ref[pl.ds(r0, SUBLANES), :]
        for sh in (1, 2, 4):
            keep = row >= sh
            a_sh = jnp.where(keep, pltpu.roll(a, sh, 0), 1.0)
            h_sh = jnp.where(keep, pltpu.roll(hh, sh, 0), 0.0)
            hh = a * h_sh + hh
            a = a * a_sh
        hh = hh + a * hprev
        h_ref[pl.ds(r0, SUBLANES), :] = hh
        return hh[SUBLANES - 1:SUBLANES, :]

    hcar_ref[...] = lax.fori_loop(0, ts // SUBLANES, scan_group, hcar_ref[...])
    mix_ref[:, D_CONV:D_CONV + D_LRU] = (h_ref[...] * gl_ref[...]).astype(jnp.bfloat16)

    uh_ref[0:CONV_HIST, :] = uh_ref[ts:ts + CONV_HIST, :]
    lxh_ref[0:LRU_HIST, :] = lxh_ref[ts:ts + LRU_HIST, :]

    x1 = x + jnp.dot(mix_ref[...], wout_ref[...], preferred_element_type=jnp.float32)
    x1_ref[...] = x1
    hf = _rms(x1, gffn_ref[...])
    _store_token_tiles(hfbuf_ref.at[slot], hf)

    hf_hi = hf.astype(jnp.bfloat16)
    hf_lo = (hf - hf_hi.astype(jnp.float32)).astype(jnp.bfloat16)
    hi_both = jnp.dot(hf_hi, wrc_ref[...], preferred_element_type=jnp.float32)
    logits = (hi_both[:, 0:ROUTER_COLS] + hi_both[:, ROUTER_COLS:2 * ROUTER_COLS]
              + jnp.dot(hf_lo, wrh_ref[...], preferred_element_type=jnp.float32)
              + br_ref[...])
    lt = logits.T

    gl = [lt[g:g + 1, :] for g in range(N_GROUPS)]
    gmax = functools.reduce(jnp.maximum, gl)
    gidx = jnp.full((1, ts), N_GROUPS - 1, jnp.int32)
    for g in range(N_GROUPS - 2, -1, -1):
        gidx = jnp.where(gl[g] == gmax, g, gidx)
    gsum = functools.reduce(lambda p, q: p + q, [jnp.exp(v - gmax) for v in gl])
    gprob = 1.0 / gsum

    def eblock(g):
        lo = EXPERT_COL0 + g * EXPERTS_PER_GROUP
        return lt[lo:lo + EXPERTS_PER_GROUP, :]

    esel = eblock(N_GROUPS - 1)
    for g in range(N_GROUPS - 2, -1, -1):
        esel = jnp.where(gidx == g, eblock(g), esel)
    ridx = lax.broadcasted_iota(jnp.int32, (EXPERTS_PER_GROUP, ts), 0)
    m1 = jnp.max(esel, axis=0, keepdims=True)
    i1 = jnp.min(jnp.where(esel == m1, ridx, EXPERTS_PER_GROUP), axis=0, keepdims=True)
    rest = jnp.where(ridx == i1, -jnp.inf, esel)
    m2 = jnp.max(rest, axis=0, keepdims=True)
    i2 = jnp.min(jnp.where(rest == m2, ridx, EXPERTS_PER_GROUP), axis=0, keepdims=True)
    p2 = jnp.exp(m2 - m1)
    den = 1.0 + p2
    gate_ref[...] = jnp.concatenate(
        [gprob / den, gprob * p2 / den,
         jnp.zeros((SUBLANES - TOP_K, ts), jnp.float32)], axis=0)
    eid0 = gidx * EXPERTS_PER_GROUP + i1
    eid1 = gidx * EXPERTS_PER_GROUP + i2

    eio = lax.broadcasted_iota(jnp.int32, (N_EXPERTS, ts), 0)
    oh0 = jnp.where(eio == eid0, 1.0, 0.0)
    oh1 = jnp.where(eio == eid1, 1.0, 0.0)
    tri = tri_ref[...]
    pre0 = jnp.dot(oh0.astype(jnp.bfloat16), tri, preferred_element_type=jnp.float32)
    pre1 = jnp.dot(oh1.astype(jnp.bfloat16), tri, preferred_element_type=jnp.float32)
    c0 = jnp.sum(oh0, axis=1, keepdims=True)
    c1 = jnp.sum(oh1, axis=1, keepdims=True)
    before = tot_ref[:, 0:1]
    rank0 = jnp.sum(oh0 * (before + pre0), axis=0, keepdims=True)
    rank1 = jnp.sum(oh1 * (before + c0 + pre1), axis=0, keepdims=True)

    after = before + c0 + c1
    nbb = jnp.floor((before + (ROW_BLOCK - 1)) * inv_block)
    nba = jnp.floor((after + (ROW_BLOCK - 1)) * inv_block)
    new = nba - nbb
    new_wide = jnp.broadcast_to(new, (N_EXPERTS, LANES))
    newbase = nal_ref[:, 0:1] + jnp.dot(tril_ref[...], new_wide.astype(jnp.bfloat16),
                                         preferred_element_type=jnp.float32)[:, 0:1]
    curb = cur_ref[:, 0:1]
    voff = newbase - nbb

    def place(oh, p):
        def look(col):
            return jnp.sum(oh * col, axis=0, keepdims=True)
        q = jnp.floor(p * inv_block)
        blk = jnp.where(q < look(nbb), look(curb), look(voff) + q)
        return blk * ROW_BLOCK + (p - q * ROW_BLOCK)

    dest = jnp.concatenate([place(oh0, rank0), place(oh1, rank1)], axis=0).astype(jnp.int32)
    dest_ref[...] = dest
    dvec_ref[slot] = dest
    dest_copy(slot).start()

    tot_ref[...] = jnp.broadcast_to(after, (N_EXPERTS, LANES))
    cur_ref[...] = jnp.broadcast_to(jnp.where(new > 0, newbase + new - 1.0, curb),
                                    (N_EXPERTS, LANES))
    nal_new = nal_ref[...] + jnp.sum(new_wide, axis=0, keepdims=True)
    nal_ref[...] = nal_new
    nb_pad = bet_ref.shape[1]
    jcol = lax.broadcasted_iota(jnp.int32, (N_EXPERTS, nb_pad), 1).astype(jnp.float32)
    erow = lax.broadcasted_iota(jnp.int32, (N_EXPERTS, nb_pad), 0).astype(jnp.float32)
    mine = jnp.where(jcol >= newbase, 1.0, 0.0) * jnp.where(jcol < newbase + new, 1.0, 0.0)
    owner = jnp.sum(mine * erow, axis=0, keepdims=True)
    taken = jnp.sum(mine, axis=0, keepdims=True)
    bet = jnp.where(taken > 0.0, owner, bet_ref[0:1, :])
    bet_ref[...] = jnp.broadcast_to(bet, bet_ref.shape)
    be_ref[...] = jnp.broadcast_to(bet, bet_ref.shape).astype(jnp.int32)
    nu_ref[...] = nal_new[0:SUBLANES, :].astype(jnp.int32)

    scatter_wait()

    @pl.when(step == last)
    def _():
        dest_copy(slot).wait()

        def issue(t, carry):
            for k in range(TOP_K):
                scatter_copy(slot, k, t).start(priority=k)
            return carry
        lax.fori_loop(0, ts, issue, 0, unroll=DMA_UNROLL)
        scatter_wait()

        fill = after - ROW_BLOCK * jnp.floor(after * inv_block)
        zvec_ref[0] = cur_ref[...].astype(jnp.int32)
        zvec_ref[1] = jnp.broadcast_to(fill, (N_EXPERTS, LANES)).astype(jnp.int32)
        zvec_ref[2] = nal_new.astype(jnp.int32)
        info = pltpu.make_async_copy(zvec_ref, zsm_ref, isem)
        info.start()
        zsrc = hfbuf_ref.at[prev]
        zsrc[...] = jnp.zeros((ts * TOKEN_ROWS, LANES), jnp.float32)
        info.wait()

        def zero_tile(r):
            return pltpu.make_async_copy(_token_tile(zsrc, 0), _token_tile(xs_hbm, r), zsem)

        def per_expert(e, total):
            filled = zsm_ref[1, e, 0]
            lo = jnp.where(filled > 0, filled, ROW_BLOCK)
            row0 = zsm_ref[0, e, 0] * ROW_BLOCK

            def issue_zero(r, carry):
                zero_tile(row0 + r).start()
                return carry
            lax.fori_loop(lo, ROW_BLOCK, issue_zero, 0)
            return total + (ROW_BLOCK - lo)
        total = lax.fori_loop(0, N_EXPERTS, per_expert, 0)

        def drain(r, carry):
            zero_tile(0).wait()
            return carry
        lax.fori_loop(0, total, drain, 0)

        block_rows = ROW_BLOCK * TOKEN_ROWS
        assert block_rows == ts * TOKEN_ROWS

        def zero_block(blk):
            start = pl.multiple_of(blk * block_rows, block_rows)
            return pltpu.make_async_copy(zsrc, xs_hbm.at[pl.ds(start, block_rows), :], bsem)

        def issue_block(blk, carry):
            zero_block(blk).start()
            return carry
        lax.fori_loop(zsm_ref[2, 0, 0], n_blocks, issue_block, 0)

        def drain_block(blk, carry):
            zero_block(blk).wait()
            return carry
        lax.fori_loop(zsm_ref[2, 0, 0], n_blocks, drain_block, 0)


def _mixer(x2d, batch, seq, n_rows, gmix, win, cw, cb, lng, lnb, lw, lb, wg, bg, lam, wout,
           gffn, wrc, wrh, br, tri, tril):
    t = batch * seq
    ns = seq // SEQ_TILE
    nb_pad = -(-(n_rows // ROW_BLOCK) // LANES) * LANES
    tok = lambda b, s: (b * ns + s, 0)
    tokl = lambda b, s: (0, b * ns + s)
    const = lambda b, s: (0, 0)

    def full(a):
        return pl.BlockSpec(a.shape, const)

    weights = (gmix, win, cw, cb, lng, lnb, lw, lb, wg, bg, lam, wout, gffn, wrc, wrh, br,
               tri, tril)
    f32, i32 = jnp.float32, jnp.int32
    return pl.pallas_call(
        _mixer_kernel,
        grid=(batch, ns),
        in_specs=[pl.BlockSpec((SEQ_TILE, D_MODEL), tok)] + [full(a) for a in weights],
        out_specs=[
            pl.BlockSpec((SEQ_TILE, D_MODEL), tok),
            pl.BlockSpec((SUBLANES, SEQ_TILE), tokl),
            pl.BlockSpec((TOP_K, SEQ_TILE), tokl),
            pl.BlockSpec((SUBLANES, nb_pad), const),
            pl.BlockSpec((SUBLANES, LANES), const),
            pl.BlockSpec(memory_space=pl.ANY),
        ],
        out_shape=[
            jax.ShapeDtypeStruct((t, D_MODEL), f32),
            jax.ShapeDtypeStruct((SUBLANES, t), f32),
            jax.ShapeDtypeStruct((TOP_K, t), i32),
            jax.ShapeDtypeStruct((SUBLANES, nb_pad), i32),
            jax.ShapeDtypeStruct((SUBLANES, LANES), i32),
            jax.ShapeDtypeStruct((n_rows * TOKEN_ROWS, LANES), f32),
        ],
        scratch_shapes=[
            pltpu.VMEM((SEQ_TILE + CONV_HIST, D_CONV), f32),
            pltpu.VMEM((SUBLANES, SEQ_TILE + CONV_HIST - SUBLANES, D_CONV), f32),
            pltpu.VMEM((SEQ_TILE + LRU_HIST, D_LRU), f32),
            pltpu.VMEM((SEQ_TILE, D_CONV), f32),
            pltpu.VMEM((SEQ_TILE, D_LRU), f32),
            pltpu.VMEM((SEQ_TILE, D_LRU), f32),
            pltpu.VMEM((SEQ_TILE, D_LRU), f32),
            pltpu.VMEM((1, D_LRU), f32),
            pltpu.VMEM((SEQ_TILE, D_MODEL), jnp.bfloat16),
            pltpu.VMEM((N_EXPERTS, LANES), f32),
            pltpu.VMEM((N_EXPERTS, LANES), f32),
            pltpu.VMEM((N_EXPERTS, LANES), f32),
            pltpu.VMEM((SUBLANES, nb_pad), f32),
            pltpu.VMEM((2, SEQ_TILE * TOKEN_ROWS, LANES), f32),
            pltpu.VMEM((2, TOP_K, SEQ_TILE), i32),
            pltpu.SMEM((2, TOP_K, SEQ_TILE), i32),
            pltpu.VMEM((3, N_EXPERTS, LANES), i32),
            pltpu.SMEM((3, N_EXPERTS, LANES), i32),
            pltpu.SemaphoreType.DMA((2,)),
            pltpu.SemaphoreType.DMA(()),
            pltpu.SemaphoreType.DMA(()),
            pltpu.SemaphoreType.DMA(()),
            pltpu.SemaphoreType.DMA(()),
        ],
        compiler_params=pltpu.CompilerParams(
            dimension_semantics=("arbitrary", "arbitrary"),
            vmem_limit_bytes=VMEM_LIMIT),
        name="mixer_router",
    )(x2d, *weights)


def _expert_kernel(order_ref, be_ref, nu_ref, xs_ref, w1_ref, w3_ref, w2_ref, y_ref,
                   w1b_ref, w3b_ref, w2b_ref):
    del order_ref
    i = pl.program_id(0)
    used = i < nu_ref[0]

    @pl.when(used & ((i == 0) | (be_ref[i] != be_ref[jnp.maximum(i - 1, 0)])))
    def _():
        w1b_ref[...] = w1_ref[0].astype(jnp.bfloat16)
        w3b_ref[...] = w3_ref[0].astype(jnp.bfloat16)
        w2b_ref[...] = w2_ref[0].astype(jnp.bfloat16)

    @pl.when(used)
    def _():
        xb = _load_token_tiles(xs_ref, ROW_BLOCK).astype(jnp.bfloat16)
        h1 = jnp.dot(xb, w1b_ref[...], preferred_element_type=jnp.float32)
        h3 = jnp.dot(xb, w3b_ref[...], preferred_element_type=jnp.float32)
        act = (h1 * _sigmoid(h1) * h3).astype(jnp.bfloat16)
        _store_token_tiles(
            y_ref, jnp.dot(act, w2b_ref[...], preferred_element_type=jnp.float32))

    @pl.when(i >= nu_ref[0])
    def _():
        y_ref[...] = jnp.zeros_like(y_ref)


def _block_order(block_e, n_used):
    n = block_e.shape[0]
    blk = jnp.arange(n, dtype=jnp.int32)
    key = jnp.where(blk < n_used, block_e, N_EXPERTS)
    earlier = (key[None, :] < key[:, None]) | ((key[None, :] == key[:, None])
                                               & (blk[None, :] < blk[:, None]))
    pos = jnp.sum(earlier.astype(jnp.int32), axis=1)
    hit = pos[None, :] == blk[:, None]
    order = jnp.sum(jnp.where(hit, blk[None, :], 0), axis=1)
    sorted_e = jnp.sum(jnp.where(hit, key[None, :], 0), axis=1)
    return order, jnp.minimum(sorted_e, N_EXPERTS - 1)


def _experts(block_e, n_used, xs, w1, w3, w2):
    n_rows = xs.shape[0] // TOKEN_ROWS
    order, sorted_e = _block_order(block_e, n_used)
    rows = lambda i, od, be, nu: (od[i], 0)
    rows_in = lambda i, od, be, nu: (od[jnp.minimum(i, nu[0] - 1)], 0)
    wsel = lambda i, od, be, nu: (be[i], 0, 0)
    return pl.pallas_call(
        _expert_kernel,
        grid_spec=pltpu.PrefetchScalarGridSpec(
            num_scalar_prefetch=3,
            grid=(n_rows // ROW_BLOCK,),
            in_specs=[pl.BlockSpec((ROW_BLOCK * TOKEN_ROWS, LANES), rows_in),
                      pl.BlockSpec((1, D_MODEL, D_EXPERT), wsel),
                      pl.BlockSpec((1, D_MODEL, D_EXPERT), wsel),
                      pl.BlockSpec((1, D_EXPERT, D_MODEL), wsel)],
            out_specs=pl.BlockSpec((ROW_BLOCK * TOKEN_ROWS, LANES), rows),
            scratch_shapes=[pltpu.VMEM((D_MODEL, D_EXPERT), jnp.bfloat16),
                            pltpu.VMEM((D_MODEL, D_EXPERT), jnp.bfloat16),
                            pltpu.VMEM((D_EXPERT, D_MODEL), jnp.bfloat16)]),
        out_shape=jax.ShapeDtypeStruct((n_rows * TOKEN_ROWS, LANES), jnp.float32),
        compiler_params=pltpu.CompilerParams(
            dimension_semantics=("arbitrary",), vmem_limit_bytes=VMEM_LIMIT),
        name="expert_mlp",
    )(order, sorted_e, n_used, xs, w1, w3, w2)


def _tail_kernel(dcur_ref, dnext_ref, x1_ref, y_hbm, gate_ref, p_ref, wple_ref, gproj_ref,
                 gple_ref, wpg_ref, gfin_ref, o_ref, ybuf_ref, sem, *, final_norm):
    tt = TAIL_TILE
    i = pl.program_id(0)

    def tile_copy(d_ref, s, k, t, t0):
        return pltpu.make_async_copy(_token_tile(y_hbm, d_ref[k, t0 + t]),
                                     _token_tile(ybuf_ref.at[s], k * tt + t), sem.at[s])

    def wait_slot(s):
        pltpu.make_async_copy(y_hbm.at[pl.ds(0, TOP_K * tt * TOKEN_ROWS)], ybuf_ref.at[s],
                              sem.at[s]).wait()

    def issue_inline(d_ref, s, t0, part):
        for t in range(part * tt // TAIL_STAGES, (part + 1) * tt // TAIL_STAGES):
            for k in range(TOP_K):
                tile_copy(d_ref, s, k, t, t0).start(priority=k)

    def compute(s, r0, d_ref, s_next, t0_next):
        rows = slice(r0, r0 + tt)
        gpad = jnp.concatenate(
            [gate_ref[:, rows], jnp.zeros((LANES - SUBLANES, tt), jnp.float32)], axis=0)
        gt = gpad.T
        yb = ybuf_ref.at[s]
        issue_inline(d_ref, s_next, t0_next, 0)
        x2 = (x1_ref[rows, :] + gt[:, 0:1] * _load_token_tiles(yb, tt)
              + gt[:, 1:2] * _load_token_tiles(yb, tt, row0=tt * TOKEN_ROWS))
        issue_inline(d_ref, s_next, t0_next, 1)
        e = _rms(jnp.dot(p_ref[rows, :].astype(jnp.bfloat16), wple_ref[...],
                         preferred_element_type=jnp.float32), gproj_ref[...])
        hg = _rms(x2, gple_ref[...]).astype(jnp.bfloat16)
        issue_inline(d_ref, s_next, t0_next, 2)
        pg = _sigmoid(jnp.dot(hg, wpg_ref[...], preferred_element_type=jnp.float32))
        issue_inline(d_ref, s_next, t0_next, 3)
        x3 = x2 + pg * e
        o_ref[rows, :] = _rms(x3, gfin_ref[...]) if final_norm else x3

    @pl.when(i == 0)
    def _():
        def issue(t, carry):
            for k in range(TOP_K):
                tile_copy(dcur_ref, 0, k, t, 0).start()
            return carry
        lax.fori_loop(0, tt, issue, 0, unroll=DMA_UNROLL)

    wait_slot(0)
    compute(0, 0, dcur_ref, 1, tt)
    wait_slot(1)
    compute(1, tt, dnext_ref, 0, 0)

    @pl.when(i == pl.num_programs(0) - 1)
    def _():
        wait_slot(0)


def _tail(dest, x1, y, gate, p2d, wple, gproj, gple, wpg, gfin, final_norm):
    t = x1.shape[0]
    step = 2 * TAIL_TILE
    n_steps = t // step
    tok = lambda i: (i, 0)
    const = lambda i: (0, 0)

    def full(a):
        return pl.BlockSpec(a.shape, const)

    return pl.pallas_call(
        functools.partial(_tail_kernel, final_norm=final_norm),
        grid=(n_steps,),
        in_specs=[pl.BlockSpec((TOP_K, step), lambda i: (0, i), memory_space=pltpu.SMEM),
                  pl.BlockSpec((TOP_K, step), lambda i: (0, jnp.minimum(i + 1, n_steps - 1)),
                               memory_space=pltpu.SMEM),
                  pl.BlockSpec((step, D_MODEL), tok),
                  pl.BlockSpec(memory_space=pl.ANY),
                  pl.BlockSpec((SUBLANES, step), lambda i: (0, i)),
                  pl.BlockSpec((step, PLE_DIM), tok),
                  full(wple), full(gproj), full(gple), full(wpg), full(gfin)],
        out_specs=pl.BlockSpec((step, D_MODEL), tok),
        out_shape=jax.ShapeDtypeStruct((t, D_MODEL), jnp.float32),
        scratch_shapes=[pltpu.VMEM((2, TOP_K * TAIL_TILE * TOKEN_ROWS, LANES), jnp.float32),
                        pltpu.SemaphoreType.DMA((2,))],
        compiler_params=pltpu.CompilerParams(
            dimension_semantics=("arbitrary",), vmem_limit_bytes=VMEM_LIMIT),
        name="tail",
    )(dest, dest, x1, y, gate, p2d, wple, gproj, gple, wpg, gfin)


def _block_diag(w):
    h, hd, _ = w.shape
    eye = jnp.eye(h, dtype=w.dtype)
    return jnp.einsum('hij,hg->higj', w, eye).reshape(h * hd, h * hd)


def _router_weights(w_group, b_group, w_expert, b_expert):
    w = jnp.zeros((D_MODEL, ROUTER_COLS), jnp.float32)
    w = w.at[:, 0:N_GROUPS].set(w_group)
    w = w.at[:, EXPERT_COL0:EXPERT_COL0 + N_EXPERTS].set(w_expert)
    bias = jnp.zeros((1, ROUTER_COLS), jnp.float32)
    bias = bias.at[0, 0:N_GROUPS].set(b_group)
    bias = bias.at[0, EXPERT_COL0:EXPERT_COL0 + N_EXPERTS].set(b_expert)
    w_hi = w.astype(jnp.bfloat16)
    w_lo = (w - w_hi.astype(jnp.float32)).astype(jnp.bfloat16)
    return jnp.concatenate([w_hi, w_lo], axis=1), w_hi, bias


def _layer(x2d, p2d, batch, seq, final_norm, g_final, g_mix, w_in, conv_dw_w, conv_dw_b,
           conv_ln_g, conv_ln_b, lru_conv_w, lru_conv_b, lru_w_r, lru_b_r, lru_w_i, lru_b_i,
           lru_lambda, w_out, g_ffn, w_group, b_group, w_expert, b_expert, w1, w3, w2,
           g_ple, w_ple, g_ple_proj, w_ple_gate):
    bf16 = jnp.bfloat16
    t = batch * seq
    row = lambda v: v.reshape(1, -1)
    wg = jnp.concatenate([_block_diag(lru_w_r), _block_diag(lru_w_i)], axis=1).astype(bf16)
    bg = jnp.concatenate([lru_b_r, lru_b_i]).reshape(1, -1)
    wrc, wrh, br = _router_weights(w_group, b_group, w_expert, b_expert)
    ti = jnp.arange(SEQ_TILE)
    tri = (ti[:, None] < ti[None, :]).astype(bf16)
    ei = jnp.arange(N_EXPERTS)
    tril = (ei[None, :] < ei[:, None]).astype(bf16)
    n_blocks = (t * TOP_K) // ROW_BLOCK + N_EXPERTS
    n_rows = n_blocks * ROW_BLOCK

    x1, gate, dest, be, nu, xs = _mixer(
        x2d, batch, seq, n_rows, row(g_mix), w_in.astype(bf16), conv_dw_w, row(conv_dw_b),
        row(conv_ln_g), row(conv_ln_b), lru_conv_w, row(lru_conv_b), wg, bg,
        row(lru_lambda), w_out.astype(bf16), row(g_ffn), wrc, wrh, br, tri, tril)
    y = _experts(be[0, :n_blocks], nu[0, 0:1], xs, w1, w3, w2)
    return _tail(dest, x1, y, gate, p2d, w_ple.astype(bf16), row(g_ple_proj), row(g_ple),
                 w_ple_gate.astype(bf16), row(g_final), final_norm)


def kernel(x, p, g_mix, w_in, conv_dw_w, conv_dw_b, conv_ln_g, conv_ln_b, lru_conv_w,
           lru_conv_b, lru_w_r, lru_b_r, lru_w_i, lru_b_i, lru_lambda, w_out, g_ffn,
           w_group, b_group, w_expert, b_expert, w1, w3, w2, g_ple, w_ple, g_ple_proj,
           w_ple_gate, g_final):
    batch, seq, d = x.shape
    assert d == D_MODEL and seq % SEQ_TILE == 0 and SEQ_TILE == ROW_BLOCK
    assert (batch * seq) % max(2 * TAIL_TILE, ROW_BLOCK) == 0
    depth = w_in.shape[0]
    layers = (g_mix, w_in, conv_dw_w, conv_dw_b, conv_ln_g, conv_ln_b, lru_conv_w,
              lru_conv_b, lru_w_r, lru_b_r, lru_w_i, lru_b_i, lru_lambda, w_out, g_ffn,
              w_group, b_group, w_expert, b_expert, w1, w3, w2, g_ple, w_ple, g_ple_proj,
              w_ple_gate)
    x2d = x.reshape(batch * seq, d)
    for i in range(depth):
        x2d = _layer(x2d, p[i].reshape(batch * seq, PLE_DIM), batch, seq, i == depth - 1,
                     g_final, *(w[i] for w in layers))
    return x2d.reshape(batch, seq, d)
```

```python
import functools

import jax
import jax.numpy as jnp
from jax import lax
from jax.experimental import pallas as pl
from jax.experimental.pallas import tpu as pltpu

D_MODEL = 1024
D_CONV = 512
D_LRU = 512
LRU_HEADS = 8
LRU_HEAD_DIM = D_LRU // LRU_HEADS
CONV_WIDTH = 31
LRU_CONV_WIDTH = 4
LRU_C = 8.0
N_GROUPS = 4
EXPERTS_PER_GROUP = 8
N_EXPERTS = N_GROUPS * EXPERTS_PER_GROUP
TOP_K = 2
D_EXPERT = D_MODEL // 2
PLE_DIM = 256
EPS = 1e-6

SUBLANES = 8
LANES = 128

SEQ_TILE = 512
CONV_HIST = 32
LRU_HIST = SUBLANES
CONV_ROWS = 32
ROW_BLOCK = 512
DMA_UNROLL = 8
TAIL_TILE = 512
TAIL_STAGES = 4
ROUTER_COLS = LANES
EXPERT_COL0 = SUBLANES
VMEM_LIMIT = 56 * 1024 * 1024


def _sigmoid(v):
    return 1.0 / (1.0 + jnp.exp(-v))


def _rms(v, g):
    return v * lax.rsqrt(jnp.mean(v * v, axis=-1, keepdims=True) + EPS) * g


TOKEN_ROWS = D_MODEL // LANES


def _store_token_tiles(ref, v, row0=0):
    n = v.shape[0]
    for c in range(TOKEN_ROWS):
        ref[pl.ds(row0 + c, n, stride=TOKEN_ROWS), :] = v[:, c * LANES:(c + 1) * LANES]


def _load_token_tiles(ref, n, row0=0):
    return jnp.concatenate(
        [ref[pl.ds(row0 + c, n, stride=TOKEN_ROWS), :] for c in range(TOKEN_ROWS)], axis=1)


def _token_tile(ref, tile):
    start = tile * TOKEN_ROWS
    if not isinstance(tile, int):
        start = pl.multiple_of(start, TOKEN_ROWS)
    return ref.at[pl.ds(start, TOKEN_ROWS), :]


def _mixer_kernel(x_ref, gmix_ref, win_ref, cw_ref, cb_ref, lng_ref, lnb_ref,
                  lw_ref, lb_ref, wg_ref, bg_ref, lam_ref, wout_ref, gffn_ref,
                  wrc_ref, br_ref, tri_ref, tril_ref,
                  x1_ref, gate_ref, dest_ref, be_ref, nu_ref, xs_hbm,
                  uh_ref, ush_ref, lxh_ref, cv_ref, a_ref, h_ref, gl_ref, hcar_ref,
                  mix_ref, tot_ref, cur_ref, nal_ref, bet_ref, hfbuf_ref, dvec_ref,
                  dsm_ref, zvec_ref, zsm_ref, dsem, ssem, zsem, bsem, isem):
    ts = SEQ_TILE
    b = pl.program_id(0)
    s = pl.program_id(1)
    step = b * pl.num_programs(1) + s
    last = pl.num_programs(0) * pl.num_programs(1) - 1
    slot = lax.rem(step, 2)
    prev = 1 - slot
    n_blocks = xs_hbm.shape[0] // (ROW_BLOCK * TOKEN_ROWS)
    inv_block = 1.0 / ROW_BLOCK

    def dest_copy(sl):
        return pltpu.make_async_copy(dvec_ref.at[sl], dsm_ref.at[sl], dsem.at[sl])

    def scatter_copy(sl, k, t):
        return pltpu.make_async_copy(_token_tile(hfbuf_ref.at[sl], t),
                                     _token_tile(xs_hbm, dsm_ref[sl, k, t]), ssem)

    def scatter_wait():
        rows = TOP_K * ts * TOKEN_ROWS
        pltpu.make_async_copy(xs_hbm.at[pl.ds(0, rows)], xs_hbm.at[pl.ds(0, rows)], ssem).wait()

    @pl.when(s == 0)
    def _():
        uh_ref[0:CONV_HIST, :] = jnp.zeros((CONV_HIST, D_CONV), jnp.float32)
        lxh_ref[0:LRU_HIST, :] = jnp.zeros((LRU_HIST, D_LRU), jnp.float32)
        hcar_ref[...] = jnp.zeros_like(hcar_ref)

    @pl.when(step == 0)
    def _():
        tot_ref[...] = jnp.zeros_like(tot_ref)
        cur_ref[...] = jnp.zeros_like(cur_ref)
        nal_ref[...] = jnp.zeros_like(nal_ref)
        bet_ref[...] = jnp.zeros_like(bet_ref)
        hfbuf_ref[1] = jnp.zeros((ts * TOKEN_ROWS, LANES), jnp.float32)
        dvec_ref[1] = (lax.broadcasted_iota(jnp.int32, (TOP_K, ts), 0) * ts
                       + lax.broadcasted_iota(jnp.int32, (TOP_K, ts), 1))
        dest_copy(1).start()

    x = x_ref[...]
    hb = _rms(x, gmix_ref[...]).astype(jnp.bfloat16)

    def zcols(j):
        return jnp.dot(hb, win_ref[:, j * D_CONV:(j + 1) * D_CONV],
                       preferred_element_type=jnp.float32)

    uh_ref[CONV_HIST:CONV_HIST + ts, :] = zcols(0) * _sigmoid(zcols(1))
    lxh_ref[LRU_HIST:LRU_HIST + ts, :] = zcols(2)
    gl_ref[...] = jax.nn.gelu(zcols(3), approximate=True)

    span = ts + CONV_HIST - SUBLANES
    u_ext = uh_ref[...]
    for r in range(1, SUBLANES):
        ush_ref[r] = pltpu.roll(u_ext, r, 0)[SUBLANES:SUBLANES + span, :]
    dest_copy(prev).wait()
    n_chunks = ts // CONV_ROWS
    for c in range(n_chunks):
        r0 = c * CONV_ROWS
        for lc in range(D_CONV // LANES):
            ls = slice(lc * LANES, (lc + 1) * LANES)
            acc = jnp.broadcast_to(cb_ref[:, ls], (CONV_ROWS, LANES))
            for j in range(CONV_WIDTH):
                k = CONV_WIDTH - 1 - j
                off = r0 + CONV_HIST - SUBLANES - SUBLANES * (j // SUBLANES)
                if j % SUBLANES == 0:
                    u_j = uh_ref[off + SUBLANES:off + SUBLANES + CONV_ROWS, ls]
                else:
                    u_j = ush_ref[j % SUBLANES, off:off + CONV_ROWS, ls]
                acc = acc + u_j * cw_ref[k:k + 1, ls]
            cv_ref[r0:r0 + CONV_ROWS, ls] = acc
        for t in range(c * ts // n_chunks, (c + 1) * ts // n_chunks):
            for k in range(TOP_K):
                scatter_copy(prev, k, t).start(priority=k)

    cv = cv_ref[...]
    mu = jnp.mean(cv, axis=-1, keepdims=True)
    dv = cv - mu
    var = jnp.mean(dv * dv, axis=-1, keepdims=True)
    yc = dv * lax.rsqrt(var + EPS) * lng_ref[...] + lnb_ref[...]
    mix_ref[:, 0:D_CONV] = (yc * _sigmoid(yc)).astype(jnp.bfloat16)

    xr = jnp.broadcast_to(lb_ref[...], (ts, D_LRU))
    for j in range(LRU_CONV_WIDTH):
        k = LRU_CONV_WIDTH - 1 - j
        xr = xr + lxh_ref[LRU_HIST - j:LRU_HIST - j + ts, :] * lw_ref[k:k + 1, :]
    gates = jnp.dot(xr.astype(jnp.bfloat16), wg_ref[...],
                    preferred_element_type=jnp.float32) + bg_ref[...]
    r_g = _sigmoid(gates[:, 0:D_LRU])
    i_g = _sigmoid(gates[:, D_LRU:2 * D_LRU])
    nlam = -lam_ref[...]
    softplus = jnp.maximum(nlam, 0.0) + jnp.log1p(jnp.exp(-jnp.abs(nlam)))
    log_a = -LRU_C * r_g * softplus
    a_ref[...] = jnp.exp(log_a)
    th = jnp.tanh(log_a)
    h_ref[...] = jnp.sqrt(-2.0 * th / (1.0 - th)) * (i_g * xr)

    row = lax.broadcasted_iota(jnp.int32, (SUBLANES, D_LRU), 0)

    def scan_group(g, hprev):
        r0 = pl.multiple_of(g * SUBLANES, SUBLANES)
        a = a_ref[pl.ds(r0, SUBLANES), :]
        hh = h_ref[pl.ds(r0, SUBLANES), :]
        for sh in (1, 2, 4):
            keep = row >= sh
            a_sh = jnp.where(keep, pltpu.roll(a, sh, 0), 1.0)
            h_sh = jnp.where(keep, pltpu.roll(hh, sh, 0), 0.0)
            hh = a * h_sh + hh
            a = a * a_sh
        hh = hh + a * hprev
        h_ref[pl.ds(r0, SUBLANES), :] = hh
        return hh[SUBLANES - 1:SUBLANES, :]

    hcar_ref[...] = lax.fori_loop(0, ts // SUBLANES, scan_group, hcar_ref[...])
    mix_ref[:, D_CONV:D_CONV + D_LRU] = (h_ref[...] * gl_ref[...]).astype(jnp.bfloat16)

    uh_ref[0:CONV_HIST, :] = uh_ref[ts:ts + CONV_HIST, :]
    lxh_ref[0:LRU_HIST, :] = lxh_ref[ts:ts + LRU_HIST, :]

    x1 = x + jnp.dot(mix_ref[...], wout_ref[...], preferred_element_type=jnp.float32)
    x1_ref[...] = x1
    hf = _rms(x1, gffn_ref[...])
    _store_token_tiles(hfbuf_ref.at[slot], hf)

    hi_both = jnp.dot(hf.astype(jnp.bfloat16), wrc_ref[...],
                      preferred_element_type=jnp.float32)
    logits = (hi_both[:, 0:ROUTER_COLS] + hi_both[:, ROUTER_COLS:2 * ROUTER_COLS]
              + br_ref[...])
    lt = logits.T

    gl = [lt[g:g + 1, :] for g in range(N_GROUPS)]
    gmax = functools.reduce(jnp.maximum, gl)
    gidx = jnp.full((1, ts), N_GROUPS - 1, jnp.int32)
    for g in range(N_GROUPS - 2, -1, -1):
        gidx = jnp.where(gl[g] == gmax, g, gidx)
    gsum = functools.reduce(lambda p, q: p + q, [jnp.exp(v - gmax) for v in gl])
    gprob = 1.0 / gsum

    def eblock(g):
        lo = EXPERT_COL0 + g * EXPERTS_PER_GROUP
        return lt[lo:lo + EXPERTS_PER_GROUP, :]

    esel = eblock(N_GROUPS - 1)
    for g in range(N_GROUPS - 2, -1, -1):
        esel = jnp.where(gidx == g, eblock(g), esel)
    ridx = lax.broadcasted_iota(jnp.int32, (EXPERTS_PER_GROUP, ts), 0)
    m1 = jnp.max(esel, axis=0, keepdims=True)
    i1 = jnp.min(jnp.where(esel == m1, ridx, EXPERTS_PER_GROUP), axis=0, keepdims=True)
    rest = jnp.where(ridx == i1, -jnp.inf, esel)
    m2 = jnp.max(rest, axis=0, keepdims=True)
    i2 = jnp.min(jnp.where(rest == m2, ridx, EXPERTS_PER_GROUP), axis=0, keepdims=True)
    p2 = jnp.exp(m2 - m1)
    den = 1.0 + p2
    gate_ref[...] = jnp.concatenate(
        [gprob / den, gprob * p2 / den,
         jnp.zeros((SUBLANES - TOP_K, ts), jnp.float32)], axis=0)
    eid0 = gidx * EXPERTS_PER_GROUP + i1
    eid1 = gidx * EXPERTS_PER_GROUP + i2

    eio = lax.broadcasted_iota(jnp.int32, (N_EXPERTS, ts), 0)
    oh0 = jnp.where(eio == eid0, 1.0, 0.0)
    oh1 = jnp.where(eio == eid1, 1.0, 0.0)
    tri = tri_ref[...]
    pre0 = jnp.dot(oh0.astype(jnp.bfloat16), tri, preferred_element_type=jnp.float32)
    pre1 = jnp.dot(oh1.astype(jnp.bfloat16), tri, preferred_element_type=jnp.float32)
    c0 = jnp.sum(oh0, axis=1, keepdims=True)
    c1 = jnp.sum(oh1, axis=1, keepdims=True)
    before = tot_ref[:, 0:1]
    rank0 = jnp.sum(oh0 * (before + pre0), axis=0, keepdims=True)
    rank1 = jnp.sum(oh1 * (before + c0 + pre1), axis=0, keepdims=True)

    after = before + c0 + c1
    nbb = jnp.floor((before + (ROW_BLOCK - 1)) * inv_block)
    nba = jnp.floor((after + (ROW_BLOCK - 1)) * inv_block)
    new = nba - nbb
    new_wide = jnp.broadcast_to(new, (N_EXPERTS, LANES))
    newbase = nal_ref[:, 0:1] + jnp.dot(tril_ref[...], new_wide.astype(jnp.bfloat16),
                                         preferred_element_type=jnp.float32)[:, 0:1]
    curb = cur_ref[:, 0:1]
    voff = newbase - nbb

    def place(oh, p):
        def look(col):
            return jnp.sum(oh * col, axis=0, keepdims=True)
        q = jnp.floor(p * inv_block)
        blk = jnp.where(q < look(nbb), look(curb), look(voff) + q)
        return blk * ROW_BLOCK + (p - q * ROW_BLOCK)

    dest = jnp.concatenate([place(oh0, rank0), place(oh1, rank1)], axis=0).astype(jnp.int32)
    dest_ref[...] = dest
    dvec_ref[slot] = dest
    dest_copy(slot).start()

    tot_ref[...] = jnp.broadcast_to(after, (N_EXPERTS, LANES))
    cur_ref[...] = jnp.broadcast_to(jnp.where(new > 0, newbase + new - 1.0, curb),
                                    (N_EXPERTS, LANES))
    nal_new = nal_ref[...] + jnp.sum(new_wide, axis=0, keepdims=True)
    nal_ref[...] = nal_new
    nb_pad = bet_ref.shape[1]
    jcol = lax.broadcasted_iota(jnp.int32, (N_EXPERTS, nb_pad), 1).astype(jnp.float32)
    erow = lax.broadcasted_iota(jnp.int32, (N_EXPERTS, nb_pad), 0).astype(jnp.float32)
    mine = jnp.where(jcol >= newbase, 1.0, 0.0) * jnp.where(jcol < newbase + new, 1.0, 0.0)
    owner = jnp.sum(mine * erow, axis=0, keepdims=True)
    taken = jnp.sum(mine, axis=0, keepdims=True)
    bet = jnp.where(taken > 0.0, owner, bet_ref[0:1, :])
    bet_ref[...] = jnp.broadcast_to(bet, bet_ref.shape)
    be_ref[...] = jnp.broadcast_to(bet, bet_ref.shape).astype(jnp.int32)
    nu_ref[...] = nal_new[0:SUBLANES, :].astype(jnp.int32)

    scatter_wait()

    @pl.when(step == last)
    def _():
        dest_copy(slot).wait()

        def issue(t, carry):
            for k in range(TOP_K):
                scatter_copy(slot, k, t).start(priority=k)
            return carry
        lax.fori_loop(0, ts, issue, 0, unroll=DMA_UNROLL)
        scatter_wait()

        fill = after - ROW_BLOCK * jnp.floor(after * inv_block)
        zvec_ref[0] = cur_ref[...].astype(jnp.int32)
        zvec_ref[1] = jnp.broadcast_to(fill, (N_EXPERTS, LANES)).astype(jnp.int32)
        zvec_ref[2] = nal_new.astype(jnp.int32)
        info = pltpu.make_async_copy(zvec_ref, zsm_ref, isem)
        info.start()
        zsrc = hfbuf_ref.at[prev]
        zsrc[...] = jnp.zeros((ts * TOKEN_ROWS, LANES), jnp.float32)
        info.wait()

        def zero_tile(r):
            return pltpu.make_async_copy(_token_tile(zsrc, 0), _token_tile(xs_hbm, r), zsem)

        def per_expert(e, total):
            filled = zsm_ref[1, e, 0]
            lo = jnp.where(filled > 0, filled, ROW_BLOCK)
            row0 = zsm_ref[0, e, 0] * ROW_BLOCK

            def issue_zero(r, carry):
                zero_tile(row0 + r).start()
                return carry
            lax.fori_loop(lo, ROW_BLOCK, issue_zero, 0)
            return total + (ROW_BLOCK - lo)
        total = lax.fori_loop(0, N_EXPERTS, per_expert, 0)

        def drain(r, carry):
            zero_tile(0).wait()
            return carry
        lax.fori_loop(0, total, drain, 0)

        piece_rows = ts * TOKEN_ROWS
        pieces = ROW_BLOCK // ts

        def zero_piece(pc):
            start = pl.multiple_of(pc * piece_rows, piece_rows)
            return pltpu.make_async_copy(zsrc, xs_hbm.at[pl.ds(start, piece_rows), :], bsem)

        def issue_piece(pc, carry):
            zero_piece(pc).start()
            return carry
        lax.fori_loop(zsm_ref[2, 0, 0] * pieces, n_blocks * pieces, issue_piece, 0)

        def drain_piece(pc, carry):
            zero_piece(pc).wait()
            return carry
        lax.fori_loop(zsm_ref[2, 0, 0] * pieces, n_blocks * pieces, drain_piece, 0)


def _mixer(x2d, batch, seq, n_rows, gmix, win, cw, cb, lng, lnb, lw, lb, wg, bg, lam, wout,
           gffn, wrc, br, tri, tril):
    t = batch * seq
    ns = seq // SEQ_TILE
    nb_pad = -(-(n_rows // ROW_BLOCK) // LANES) * LANES
    tok = lambda b, s: (b * ns + s, 0)
    tokl = lambda b, s: (0, b * ns + s)
    const = lambda b, s: (0, 0)

    def full(a):
        return pl.BlockSpec(a.shape, const)

    weights = (gmix, win, cw, cb, lng, lnb, lw, lb, wg, bg, lam, wout, gffn, wrc, br,
               tri, tril)
    f32, i32 = jnp.float32, jnp.int32
    return pl.pallas_call(
        _mixer_kernel,
        grid=(batch, ns),
        in_specs=[pl.BlockSpec((SEQ_TILE, D_MODEL), tok)] + [full(a) for a in weights],
        out_specs=[
            pl.BlockSpec((SEQ_TILE, D_MODEL), tok),
            pl.BlockSpec((SUBLANES, SEQ_TILE), tokl),
            pl.BlockSpec((TOP_K, SEQ_TILE), tokl),
            pl.BlockSpec((SUBLANES, nb_pad), const),
            pl.BlockSpec((SUBLANES, LANES), const),
            pl.BlockSpec(memory_space=pl.ANY),
        ],
        out_shape=[
            jax.ShapeDtypeStruct((t, D_MODEL), f32),
            jax.ShapeDtypeStruct((SUBLANES, t), f32),
            jax.ShapeDtypeStruct((TOP_K, t), i32),
            jax.ShapeDtypeStruct((SUBLANES, nb_pad), i32),
            jax.ShapeDtypeStruct((SUBLANES, LANES), i32),
            jax.ShapeDtypeStruct((n_rows * TOKEN_ROWS, LANES), f32),
        ],
        scratch_shapes=[
            pltpu.VMEM((SEQ_TILE + CONV_HIST, D_CONV), f32),
            pltpu.VMEM((SUBLANES, SEQ_TILE + CONV_HIST - SUBLANES, D_CONV), f32),
            pltpu.VMEM((SEQ_TILE + LRU_HIST, D_LRU), f32),
            pltpu.VMEM((SEQ_TILE, D_CONV), f32),
            pltpu.VMEM((SEQ_TILE, D_LRU), f32),
            pltpu.VMEM((SEQ_TILE, D_LRU), f32),
            pltpu.VMEM((SEQ_TILE, D_LRU), f32),
            pltpu.VMEM((1, D_LRU), f32),
            pltpu.VMEM((SEQ_TILE, D_MODEL), jnp.bfloat16),
            pltpu.VMEM((N_EXPERTS, LANES), f32),
            pltpu.VMEM((N_EXPERTS, LANES), f32),
            pltpu.VMEM((N_EXPERTS, LANES), f32),
            pltpu.VMEM((SUBLANES, nb_pad), f32),
            pltpu.VMEM((2, SEQ_TILE * TOKEN_ROWS, LANES), f32),
            pltpu.VMEM((2, TOP_K, SEQ_TILE), i32),
            pltpu.SMEM((2, TOP_K, SEQ_TILE), i32),
            pltpu.VMEM((3, N_EXPERTS, LANES), i32),
            pltpu.SMEM((3, N_EXPERTS, LANES), i32),
            pltpu.SemaphoreType.DMA((2,)),
            pltpu.SemaphoreType.DMA(()),
            pltpu.SemaphoreType.DMA(()),
            pltpu.SemaphoreType.DMA(()),
            pltpu.SemaphoreType.DMA(()),
        ],
        compiler_params=pltpu.CompilerParams(
            dimension_semantics=("arbitrary", "arbitrary"),
            vmem_limit_bytes=VMEM_LIMIT),
        name="mixer_router",
    )(x2d, *weights)


def _expert_kernel(order_ref, be_ref, nu_ref, xs_ref, w1_ref, w3_ref, w2_ref, y_ref,
                   w1b_ref, w3b_ref, w2b_ref):
    del order_ref
    i = pl.program_id(0)
    used = i < nu_ref[0]

    @pl.when(used & ((i == 0) | (be_ref[i] != be_ref[jnp.maximum(i - 1, 0)])))
    def _():
        w1b_ref[...] = w1_ref[0].astype(jnp.bfloat16)
        w3b_ref[...] = w3_ref[0].astype(jnp.bfloat16)
        w2b_ref[...] = w2_ref[0].astype(jnp.bfloat16)

    @pl.when(used)
    def _():
        xb = _load_token_tiles(xs_ref, ROW_BLOCK).astype(jnp.bfloat16)
        h1 = jnp.dot(xb, w1b_ref[...], preferred_element_type=jnp.float32)
        h3 = jnp.dot(xb, w3b_ref[...], preferred_element_type=jnp.float32)
        act = (h1 * _sigmoid(h1) * h3).astype(jnp.bfloat16)
        _store_token_tiles(
            y_ref, jnp.dot(act, w2b_ref[...], preferred_element_type=jnp.float32))

    @pl.when(i >= nu_ref[0])
    def _():
        y_ref[...] = jnp.zeros_like(y_ref)


def _block_order(block_e, n_used):
    n = block_e.shape[0]
    blk = jnp.arange(n, dtype=jnp.int32)
    key = jnp.where(blk < n_used, block_e, N_EXPERTS)
    earlier = (key[None, :] < key[:, None]) | ((key[None, :] == key[:, None])
                                               & (blk[None, :] < blk[:, None]))
    pos = jnp.sum(earlier.astype(jnp.int32), axis=1)
    hit = pos[None, :] == blk[:, None]
    order = jnp.sum(jnp.where(hit, blk[None, :], 0), axis=1)
    sorted_e = jnp.sum(jnp.where(hit, key[None, :], 0), axis=1)
    return order, jnp.minimum(sorted_e, N_EXPERTS - 1)


def _experts(block_e, n_used, xs, w1, w3, w2):
    n_rows = xs.shape[0] // TOKEN_ROWS
    order, sorted_e = _block_order(block_e, n_used)
    rows = lambda i, od, be, nu: (od[i], 0)
    rows_in = lambda i, od, be, nu: (od[jnp.minimum(i, nu[0] - 1)], 0)
    wsel = lambda i, od, be, nu: (be[i], 0, 0)
    return pl.pallas_call(
        _expert_kernel,
        grid_spec=pltpu.PrefetchScalarGridSpec(
            num_scalar_prefetch=3,
            grid=(n_rows // ROW_BLOCK,),
            in_specs=[pl.BlockSpec((ROW_BLOCK * TOKEN_ROWS, LANES), rows_in),
                      pl.BlockSpec((1, D_MODEL, D_EXPERT), wsel),
                      pl.BlockSpec((1, D_MODEL, D_EXPERT), wsel),
                      pl.BlockSpec((1, D_EXPERT, D_MODEL), wsel)],
            out_specs=pl.BlockSpec((ROW_BLOCK * TOKEN_ROWS, LANES), rows),
            scratch_shapes=[pltpu.VMEM((D_MODEL, D_EXPERT), jnp.bfloat16),
                            pltpu.VMEM((D_MODEL, D_EXPERT), jnp.bfloat16),
                            pltpu.VMEM((D_EXPERT, D_MODEL), jnp.bfloat16)]),
        out_shape=jax.ShapeDtypeStruct((n_rows * TOKEN_ROWS, LANES), jnp.float32),
        compiler_params=pltpu.CompilerParams(
            dimension_semantics=("arbitrary",), vmem_limit_bytes=VMEM_LIMIT),
        name="expert_mlp",
    )(order, sorted_e, n_used, xs, w1, w3, w2)


def _tail_kernel(dcur_ref, dnext_ref, x1_ref, y_hbm, gate_ref, p_ref, wple_ref, gproj_ref,
                 gple_ref, wpg_ref, gfin_ref, o_ref, ybuf_ref, sem, *, final_norm):
    tt = TAIL_TILE
    i = pl.program_id(0)

    def tile_copy(d_ref, s, k, t, t0):
        return pltpu.make_async_copy(_token_tile(y_hbm, d_ref[k, t0 + t]),
                                     _token_tile(ybuf_ref.at[s], k * tt + t), sem.at[s])

    def wait_slot(s):
        pltpu.make_async_copy(y_hbm.at[pl.ds(0, TOP_K * tt * TOKEN_ROWS)], ybuf_ref.at[s],
                              sem.at[s]).wait()

    def issue_inline(d_ref, s, t0, part):
        for t in range(part * tt // TAIL_STAGES, (part + 1) * tt // TAIL_STAGES):
            for k in range(TOP_K):
                tile_copy(d_ref, s, k, t, t0).start(priority=k)

    def compute(s, r0, d_ref, s_next, t0_next):
        rows = slice(r0, r0 + tt)
        gpad = jnp.concatenate(
            [gate_ref[:, rows], jnp.zeros((LANES - SUBLANES, tt), jnp.float32)], axis=0)
        gt = gpad.T
        yb = ybuf_ref.at[s]
        issue_inline(d_ref, s_next, t0_next, 0)
        x2 = (x1_ref[rows, :] + gt[:, 0:1] * _load_token_tiles(yb, tt)
              + gt[:, 1:2] * _load_token_tiles(yb, tt, row0=tt * TOKEN_ROWS))
        issue_inline(d_ref, s_next, t0_next, 1)
        e = _rms(jnp.dot(p_ref[rows, :].astype(jnp.bfloat16), wple_ref[...],
                         preferred_element_type=jnp.float32), gproj_ref[...])
        hg = _rms(x2, gple_ref[...]).astype(jnp.bfloat16)
        issue_inline(d_ref, s_next, t0_next, 2)
        pg = _sigmoid(jnp.dot(hg, wpg_ref[...], preferred_element_type=jnp.float32))
        issue_inline(d_ref, s_next, t0_next, 3)
        x3 = x2 + pg * e
        o_ref[rows, :] = _rms(x3, gfin_ref[...]) if final_norm else x3

    @pl.when(i == 0)
    def _():
        def issue(t, carry):
            for k in range(TOP_K):
                tile_copy(dcur_ref, 0, k, t, 0).start()
            return carry
        lax.fori_loop(0, tt, issue, 0, unroll=DMA_UNROLL)

    wait_slot(0)
    compute(0, 0, dcur_ref, 1, tt)
    wait_slot(1)
    compute(1, tt, dnext_ref, 0, 0)

    @pl.when(i == pl.num_programs(0) - 1)
    def _():
        wait_slot(0)


def _tail(dest, x1, y, gate, p2d, wple, gproj, gple, wpg, gfin, final_norm):
    t = x1.shape[0]
    step = 2 * TAIL_TILE
    n_steps = t // step
    tok = lambda i: (i, 0)
    const = lambda i: (0, 0)

    def full(a):
        return pl.BlockSpec(a.shape, const)

    return pl.pallas_call(
        functools.partial(_tail_kernel, final_norm=final_norm),
        grid=(n_steps,),
        in_specs=[pl.BlockSpec((TOP_K, step), lambda i: (0, i), memory_space=pltpu.SMEM),
                  pl.BlockSpec((TOP_K, step), lambda i: (0, jnp.minimum(i + 1, n_steps - 1)),
                               memory_space=pltpu.SMEM),
                  pl.BlockSpec((step, D_MODEL), tok),
                  pl.BlockSpec(memory_space=pl.ANY),
                  pl.BlockSpec((SUBLANES, step), lambda i: (0, i)),
                  pl.BlockSpec((step, PLE_DIM), tok),
                  full(wple), full(gproj), full(gple), full(wpg), full(gfin)],
        out_specs=pl.BlockSpec((step, D_MODEL), tok),
        out_shape=jax.ShapeDtypeStruct((t, D_MODEL), jnp.float32),
        scratch_shapes=[pltpu.VMEM((2, TOP_K * TAIL_TILE * TOKEN_ROWS, LANES), jnp.float32),
                        pltpu.SemaphoreType.DMA((2,))],
        compiler_params=pltpu.CompilerParams(
            dimension_semantics=("arbitrary",), vmem_limit_bytes=VMEM_LIMIT),
        name="tail",
    )(dest, dest, x1, y, gate, p2d, wple, gproj, gple, wpg, gfin)


def _block_diag(w):
    h, hd, _ = w.shape
    eye = jnp.eye(h, dtype=w.dtype)
    return jnp.einsum('hij,hg->higj', w, eye).reshape(h * hd, h * hd)


def _router_weights(w_group, b_group, w_expert, b_expert):
    w = jnp.zeros((D_MODEL, ROUTER_COLS), jnp.float32)
    w = w.at[:, 0:N_GROUPS].set(w_group)
    w = w.at[:, EXPERT_COL0:EXPERT_COL0 + N_EXPERTS].set(w_expert)
    bias = jnp.zeros((1, ROUTER_COLS), jnp.float32)
    bias = bias.at[0, 0:N_GROUPS].set(b_group)
    bias = bias.at[0, EXPERT_COL0:EXPERT_COL0 + N_EXPERTS].set(b_expert)
    w_hi = w.astype(jnp.bfloat16)
    w_lo = (w - w_hi.astype(jnp.float32)).astype(jnp.bfloat16)
    return jnp.concatenate([w_hi, w_lo], axis=1), bias


def _layer(x2d, p2d, batch, seq, final_norm, g_final, g_mix, w_in, conv_dw_w, conv_dw_b,
           conv_ln_g, conv_ln_b, lru_conv_w, lru_conv_b, lru_w_r, lru_b_r, lru_w_i, lru_b_i,
           lru_lambda, w_out, g_ffn, w_group, b_group, w_expert, b_expert, w1, w3, w2,
           g_ple, w_ple, g_ple_proj, w_ple_gate):
    bf16 = jnp.bfloat16
    t = batch * seq
    row = lambda v: v.reshape(1, -1)
    wg = jnp.concatenate([_block_diag(lru_w_r), _block_diag(lru_w_i)], axis=1).astype(bf16)
    bg = jnp.concatenate([lru_b_r, lru_b_i]).reshape(1, -1)
    wrc, br = _router_weights(w_group, b_group, w_expert, b_expert)
    ti = jnp.arange(SEQ_TILE)
    tri = (ti[:, None] < ti[None, :]).astype(bf16)
    ei = jnp.arange(N_EXPERTS)
    tril = (ei[None, :] < ei[:, None]).astype(bf16)
    n_blocks = (t * TOP_K) // ROW_BLOCK + N_EXPERTS
    n_rows = n_blocks * ROW_BLOCK

    x1, gate, dest, be, nu, xs = _mixer(
        x2d, batch, seq, n_rows, row(g_mix), w_in.astype(bf16), conv_dw_w, row(conv_dw_b),
        row(conv_ln_g), row(conv_ln_b), lru_conv_w, row(lru_conv_b), wg, bg,
        row(lru_lambda), w_out.astype(bf16), row(g_ffn), wrc, br, tri, tril)
    y = _experts(be[0, :n_blocks], nu[0, 0:1], xs, w1, w3, w2)
    return _tail(dest, x1, y, gate, p2d, w_ple.astype(bf16), row(g_ple_proj), row(g_ple),
                 w_ple_gate.astype(bf16), row(g_final), final_norm)


def kernel(x, p, g_mix, w_in, conv_dw_w, conv_dw_b, conv_ln_g, conv_ln_b, lru_conv_w,
           lru_conv_b, lru_w_r, lru_b_r, lru_w_i, lru_b_i, lru_lambda, w_out, g_ffn,
           w_group, b_group, w_expert, b_expert, w1, w3, w2, g_ple, w_ple, g_ple_proj,
           w_ple_gate, g_final):
    batch, seq, d = x.shape
    assert d == D_MODEL and seq % SEQ_TILE == 0 and ROW_BLOCK % SEQ_TILE == 0
    assert (batch * seq) % max(2 * TAIL_TILE, ROW_BLOCK) == 0
    depth = w_in.shape[0]
    layers = (g_mix, w_in, conv_dw_w, conv_dw_b, conv_ln_g, conv_ln_b, lru_conv_w,
              lru_conv_b, lru_w_r, lru_b_r, lru_w_i, lru_b_i, lru_lambda, w_out, g_ffn,
              w_group, b_group, w_expert, b_expert, w1, w3, w2, g_ple, w_ple, g_ple_proj,
              w_ple_gate)
    x2d = x.reshape(batch * seq, d)
    for i in range(depth):
        x2d = _layer(x2d, p[i].reshape(batch * seq, PLE_DIM), batch, seq, i == depth - 1,
                     g_final, *(w[i] for w in layers))
    return x2d.reshape(batch, seq, d)
```
